```python
import jax, jax.numpy as jnp
from jax import lax
import numpy as np

D_MODEL = 2048
BATCH = 8
SEQ = 4096
DEPTH = 4

N_MIXERS = 3
N_HEADS = 16
HEAD_DIM = D_MODEL // N_HEADS
Q_BLOCK = 128
POOL_WINDOWS = (2, 4, 8, 16)
N_POOL_GROUPS = len(POOL_WINDOWS)
POOL_GROUP = D_MODEL // N_POOL_GROUPS
CONV_W = 3
D_FF = ((8 * D_MODEL // 3 + 255) // 256) * 256
EPS = 1e-6
N_SB = (DEPTH + 2) // 3
N_POOL = (DEPTH + 1) // 3
N_CONV = DEPTH // 3

kernel_name = "hybrid_stickbreak_pool_shortconv_trunk"


def rmsnorm(x, g):
    xf = x.astype(jnp.float32)
    y = xf * lax.rsqrt(jnp.mean(xf * xf, axis=-1, keepdims=True) + EPS)
    return (y * g.astype(jnp.float32)).astype(x.dtype)


def head_rmsnorm(x, g):
    xf = x.astype(jnp.float32)
    return xf * lax.rsqrt(jnp.mean(xf * xf, axis=-1, keepdims=True) + EPS) * g.astype(jnp.float32)


def stick_breaking_attention(h, w_qkv, g_q, g_k, w_o):
    B, S, D = h.shape
    qkv = h @ w_qkv
    q, k, v = jnp.split(qkv, 3, axis=-1)
    q = head_rmsnorm(q.reshape(B, S, N_HEADS, HEAD_DIM), g_q)
    k = head_rmsnorm(k.reshape(B, S, N_HEADS, HEAD_DIM), g_k)
    v = v.reshape(B, S, N_HEADS, HEAD_DIM).astype(jnp.float32)
    scale = HEAD_DIM ** -0.5
    outs = []
    for blk in range(S // Q_BLOCK):
        t0 = blk * Q_BLOCK
        kl = t0 + Q_BLOCK
        z = jnp.einsum('bthd,bshd->bhts', q[:, t0:kl], k[:, :kl]) * scale
        t_idx = t0 + jnp.arange(Q_BLOCK)[:, None]
        s_idx = jnp.arange(kl)[None, :]
        mask = s_idx < t_idx
        log_keep = jnp.where(mask, jax.nn.log_sigmoid(-z), 0.0)
        after = lax.cumsum(log_keep, axis=3, reverse=True) - log_keep
        log_a = jax.nn.log_sigmoid(z) + after
        a = jnp.where(mask, jnp.exp(log_a), 0.0)
        o = jnp.einsum('bhts,bshd->bthd', a, v[:, :kl])
        outs.append(o.astype(h.dtype))
    o = jnp.concatenate(outs, axis=1).reshape(B, S, D)
    return o @ w_o


def multiscale_pool(h, w_pool, scale):
    B, S, D = h.shape
    hg = h.astype(jnp.float32).reshape(B, S, N_POOL_GROUPS, POOL_GROUP)
    c = jnp.cumsum(hg, axis=1)
    pos1 = jnp.arange(1, S + 1)
    pieces = []
    for g, w in enumerate(POOL_WINDOWS):
        cg = c[:, :, g]
        lag = jnp.pad(cg, ((0, 0), (w, 0), (0, 0)))[:, :S]
        cnt = jnp.minimum(pos1, w).astype(jnp.float32)[None, :, None]
        pieces.append((cg - lag) / cnt - hg[:, :, g])
    p = jnp.stack(pieces, axis=2).astype(h.dtype)
    y = jnp.einsum('bsgc,gcd->bsgd', p, w_pool).reshape(B, S, D)
    return y * scale


def short_gated_conv(h, w_in, w_conv, w_out):
    D = h.shape[-1]
    bcx = h @ w_in
    b, c, u = jnp.split(bcx, 3, axis=-1)
    g = c * u
    y = lax.conv_general_dilated(
        g, w_conv[:, None, :].astype(g.dtype), window_strides=(1,),
        padding=[(CONV_W - 1, 0)], dimension_numbers=('NWC', 'WIO', 'NWC'),
        feature_group_count=D)
    return (b * y) @ w_out


def swiglu(h, w_gate, w_up, w_down):
    return (jax.nn.silu(h @ w_gate) * (h @ w_up)) @ w_down


def _fwd_setup_inputs(seed: int = 0) -> dict:
    key = jax.random.key(seed)
    ks = jax.random.split(key, 16)
    f32 = jnp.float32
    D, F = D_MODEL, D_FF
    def nrm(k, shape, s):
        return jax.random.normal(k, shape, f32) * s
    return {
        "x": jax.random.normal(ks[0], (BATCH, SEQ, D), f32),
        "norm_mix_g": 1.0 + nrm(ks[1], (DEPTH, D), 0.02),
        "norm_ffn_g": 1.0 + nrm(ks[2], (DEPTH, D), 0.02),
        "sb_w_qkv": nrm(ks[3], (N_SB, D, 3 * D), D ** -0.5),
        "sb_g_q": 1.0 + nrm(ks[4], (N_SB, HEAD_DIM), 0.02),
        "sb_g_k": 1.0 + nrm(ks[5], (N_SB, HEAD_DIM), 0.02),
        "sb_w_o": nrm(ks[6], (N_SB, D, D), D ** -0.5),
        "pool_w": nrm(ks[7], (N_POOL, N_POOL_GROUPS, POOL_GROUP, POOL_GROUP), POOL_GROUP ** -0.5),
        "pool_scale": 1.0 + nrm(ks[8], (N_POOL, D), 0.02),
        "conv_w_in": nrm(ks[9], (N_CONV, D, 3 * D), D ** -0.5),
        "conv_w": nrm(ks[10], (N_CONV, CONV_W, D), CONV_W ** -0.5),
        "conv_w_out": nrm(ks[11], (N_CONV, D, D), D ** -0.5),
        "ffn_w_gate": nrm(ks[12], (DEPTH, D, F), D ** -0.5),
        "ffn_w_up": nrm(ks[13], (DEPTH, D, F), D ** -0.5),
        "ffn_w_down": nrm(ks[14], (DEPTH, F, D), F ** -0.5),
    }


def _fwd_reference(x, norm_mix_g, norm_ffn_g, sb_w_qkv, sb_g_q, sb_g_k, sb_w_o,
              pool_w, pool_scale, conv_w_in, conv_w, conv_w_out,
              ffn_w_gate, ffn_w_up, ffn_w_down):
    for i in range(DEPTH):
        kind, j = i % N_MIXERS, i // N_MIXERS
        h = rmsnorm(x, norm_mix_g[i])
        if kind == 0:
            x = x + stick_breaking_attention(h, sb_w_qkv[j], sb_g_q[j], sb_g_k[j], sb_w_o[j])
        elif kind == 1:
            x = x + multiscale_pool(h, pool_w[j], pool_scale[j])
        else:
            x = x + short_gated_conv(h, conv_w_in[j], conv_w[j], conv_w_out[j])
        h = rmsnorm(x, norm_ffn_g[i])
        x = x + swiglu(h, ffn_w_gate[i], ffn_w_up[i], ffn_w_down[i])
    return x


import jax as _jax
import jax.numpy as _jnp

TWIN_FORMAT = 'train_step'
FWD_PARAMS = ['x', 'norm_mix_g', 'norm_ffn_g', 'sb_w_qkv', 'sb_g_q', 'sb_g_k', 'sb_w_o', 'pool_w', 'pool_scale', 'conv_w_in', 'conv_w', 'conv_w_out', 'ffn_w_gate', 'ffn_w_up', 'ffn_w_down']
TWIN_WEIGHTS = ['norm_mix_g', 'norm_ffn_g', 'sb_w_qkv', 'sb_g_q', 'sb_g_k', 'sb_w_o', 'pool_w', 'pool_scale', 'conv_w_in', 'conv_w', 'conv_w_out', 'ffn_w_gate', 'ffn_w_up', 'ffn_w_down']
TWIN_DIFF_INPUT = 'x'
TWIN_INPUTS = ['x', 'norm_mix_g', 'norm_ffn_g', 'sb_w_qkv', 'sb_g_q', 'sb_g_k', 'sb_w_o', 'pool_w', 'pool_scale', 'conv_w_in', 'conv_w', 'conv_w_out', 'ffn_w_gate', 'ffn_w_up', 'ffn_w_down', 'loss_target', 'm_norm_mix_g', 'm_norm_ffn_g', 'm_sb_w_qkv', 'm_sb_g_q', 'm_sb_g_k', 'm_sb_w_o', 'm_pool_w', 'm_pool_scale', 'm_conv_w_in', 'm_conv_w', 'm_conv_w_out', 'm_ffn_w_gate', 'm_ffn_w_up', 'm_ffn_w_down', 'v_norm_mix_g', 'v_norm_ffn_g', 'v_sb_w_qkv', 'v_sb_g_q', 'v_sb_g_k', 'v_sb_w_o', 'v_pool_w', 'v_pool_scale', 'v_conv_w_in', 'v_conv_w', 'v_conv_w_out', 'v_ffn_w_gate', 'v_ffn_w_up', 'v_ffn_w_down']
TWIN_OUTPUTS = ['loss', 'grad_x', 'grad_norm_mix_g', 'grad_norm_ffn_g', 'grad_sb_w_qkv', 'grad_sb_g_q', 'grad_sb_g_k', 'grad_sb_w_o', 'grad_pool_w', 'grad_pool_scale', 'grad_conv_w_in', 'grad_conv_w', 'grad_conv_w_out', 'grad_ffn_w_gate', 'grad_ffn_w_up', 'grad_ffn_w_down', 'delta_norm_mix_g', 'delta_norm_ffn_g', 'delta_sb_w_qkv', 'delta_sb_g_q', 'delta_sb_g_k', 'delta_sb_w_o', 'delta_pool_w', 'delta_pool_scale', 'delta_conv_w_in', 'delta_conv_w', 'delta_conv_w_out', 'delta_ffn_w_gate', 'delta_ffn_w_up', 'delta_ffn_w_down', 'new_m_norm_mix_g', 'new_m_norm_ffn_g', 'new_m_sb_w_qkv', 'new_m_sb_g_q', 'new_m_sb_g_k', 'new_m_sb_w_o', 'new_m_pool_w', 'new_m_pool_scale', 'new_m_conv_w_in', 'new_m_conv_w', 'new_m_conv_w_out', 'new_m_ffn_w_gate', 'new_m_ffn_w_up', 'new_m_ffn_w_down', 'new_v_norm_mix_g', 'new_v_norm_ffn_g', 'new_v_sb_w_qkv', 'new_v_sb_g_q', 'new_v_sb_g_k', 'new_v_sb_w_o', 'new_v_pool_w', 'new_v_pool_scale', 'new_v_conv_w_in', 'new_v_conv_w', 'new_v_conv_w_out', 'new_v_ffn_w_gate', 'new_v_ffn_w_up', 'new_v_ffn_w_down']
TWIN_LEAF_KINDS = {'loss': 'loss', 'grad_x': 'grad_x', 'grad_norm_mix_g': 'grad_w', 'grad_norm_ffn_g': 'grad_w', 'grad_sb_w_qkv': 'grad_w', 'grad_sb_g_q': 'grad_w', 'grad_sb_g_k': 'grad_w', 'grad_sb_w_o': 'grad_w', 'grad_pool_w': 'grad_w', 'grad_pool_scale': 'grad_w', 'grad_conv_w_in': 'grad_w', 'grad_conv_w': 'grad_w', 'grad_conv_w_out': 'grad_w', 'grad_ffn_w_gate': 'grad_w', 'grad_ffn_w_up': 'grad_w', 'grad_ffn_w_down': 'grad_w', 'delta_norm_mix_g': 'delta_w', 'delta_norm_ffn_g': 'delta_w', 'delta_sb_w_qkv': 'delta_w', 'delta_sb_g_q': 'delta_w', 'delta_sb_g_k': 'delta_w', 'delta_sb_w_o': 'delta_w', 'delta_pool_w': 'delta_w', 'delta_pool_scale': 'delta_w', 'delta_conv_w_in': 'delta_w', 'delta_conv_w': 'delta_w', 'delta_conv_w_out': 'delta_w', 'delta_ffn_w_gate': 'delta_w', 'delta_ffn_w_up': 'delta_w', 'delta_ffn_w_down': 'delta_w', 'new_m_norm_mix_g': 'new_m', 'new_m_norm_ffn_g': 'new_m', 'new_m_sb_w_qkv': 'new_m', 'new_m_sb_g_q': 'new_m', 'new_m_sb_g_k': 'new_m', 'new_m_sb_w_o': 'new_m', 'new_m_pool_w': 'new_m', 'new_m_pool_scale': 'new_m', 'new_m_conv_w_in': 'new_m', 'new_m_conv_w': 'new_m', 'new_m_conv_w_out': 'new_m', 'new_m_ffn_w_gate': 'new_m', 'new_m_ffn_w_up': 'new_m', 'new_m_ffn_w_down': 'new_m', 'new_v_norm_mix_g': 'new_v', 'new_v_norm_ffn_g': 'new_v', 'new_v_sb_w_qkv': 'new_v', 'new_v_sb_g_q': 'new_v', 'new_v_sb_g_k': 'new_v', 'new_v_sb_w_o': 'new_v', 'new_v_pool_w': 'new_v', 'new_v_pool_scale': 'new_v', 'new_v_conv_w_in': 'new_v', 'new_v_conv_w': 'new_v', 'new_v_conv_w_out': 'new_v', 'new_v_ffn_w_gate': 'new_v', 'new_v_ffn_w_up': 'new_v', 'new_v_ffn_w_down': 'new_v'}


def _forward(args):
    return _fwd_reference(*[args[k] for k in FWD_PARAMS])


def _output_shape():
    def fwd():
        inp = _fwd_setup_inputs(0)
        return _fwd_reference(*[inp[k] for k in FWD_PARAMS])
    out = _jax.eval_shape(fwd)
    return out.shape, out.dtype

N_MICROBATCH = 1
ADAM_LR = 0.001
ADAM_B1 = 0.9
ADAM_B2 = 0.999
ADAM_EPS = 1e-08
ADAM_WD = 0.01
ADAM_STEP = 10
PER_EXAMPLE_BATCH_AXIS = {'x': 0, 'loss_target': 0}
SHARED_INPUTS = []
_WEIGHT_DTYPES = {'norm_mix_g': _jnp.float32, 'norm_ffn_g': _jnp.float32, 'sb_w_qkv': _jnp.float32, 'sb_g_q': _jnp.float32, 'sb_g_k': _jnp.float32, 'sb_w_o': _jnp.float32, 'pool_w': _jnp.float32, 'pool_scale': _jnp.float32, 'conv_w_in': _jnp.float32, 'conv_w': _jnp.float32, 'conv_w_out': _jnp.float32, 'ffn_w_gate': _jnp.float32, 'ffn_w_up': _jnp.float32, 'ffn_w_down': _jnp.float32}
MOMENT_SCALE = {'norm_mix_g': 2.516257e+01, 'norm_ffn_g': 1.234190e+01, 'sb_w_qkv': 3.146482e-01, 'sb_g_q': 1.592120e+01, 'sb_g_k': 1.589160e+01, 'sb_w_o': 4.618592e-01, 'pool_w': 8.959234e-01, 'pool_scale': 1.220549e+01, 'conv_w_in': 5.726778e-01, 'conv_w': 9.183889e+00, 'conv_w_out': 4.431302e-01, 'ffn_w_gate': 1.833053e-01, 'ffn_w_up': 1.904247e-01, 'ffn_w_down': 3.133104e-01}


def _to_microbatches(a, axis):
    t = _jnp.moveaxis(a, axis, 0)
    t = t.reshape((N_MICROBATCH, t.shape[0] // N_MICROBATCH) + t.shape[1:])
    return _jnp.moveaxis(t, 1, axis + 1)


def setup_inputs(seed: int = 0) -> dict:
    inp = _fwd_setup_inputs(seed)
    key = _jax.random.fold_in(_jax.random.key(seed), 7919)
    shape, _ = _output_shape()
    out = dict(inp)
    out["loss_target"] = _jax.random.normal(_jax.random.fold_in(key, 0), shape, _jnp.float32)
    for i, name in enumerate(TWIN_WEIGHTS):
        w = inp[name].astype(_jnp.float32)
        if MOMENT_SCALE is None:
            s = _jnp.sqrt(_jnp.mean(_jnp.square(w)) + 1e-30)
        else:
            s = MOMENT_SCALE[name]
        km, kv = _jax.random.split(_jax.random.fold_in(key, i + 1))
        out[name] = w
        out["m_" + name] = s * _jax.random.normal(km, w.shape, _jnp.float32)
        out["v_" + name] = (s * s) * _jax.random.uniform(kv, w.shape, _jnp.float32, 0.5, 1.5)
    if N_MICROBATCH > 1:
        for name, axis in PER_EXAMPLE_BATCH_AXIS.items():
            out[name] = _to_microbatches(out[name], axis)
    return {'x': out['x'], 'norm_mix_g': out['norm_mix_g'], 'norm_ffn_g': out['norm_ffn_g'], 'sb_w_qkv': out['sb_w_qkv'], 'sb_g_q': out['sb_g_q'], 'sb_g_k': out['sb_g_k'], 'sb_w_o': out['sb_w_o'], 'pool_w': out['pool_w'], 'pool_scale': out['pool_scale'], 'conv_w_in': out['conv_w_in'], 'conv_w': out['conv_w'], 'conv_w_out': out['conv_w_out'], 'ffn_w_gate': out['ffn_w_gate'], 'ffn_w_up': out['ffn_w_up'], 'ffn_w_down': out['ffn_w_down'], 'loss_target': out['loss_target'], 'm_norm_mix_g': out['m_norm_mix_g'], 'm_norm_ffn_g': out['m_norm_ffn_g'], 'm_sb_w_qkv': out['m_sb_w_qkv'], 'm_sb_g_q': out['m_sb_g_q'], 'm_sb_g_k': out['m_sb_g_k'], 'm_sb_w_o': out['m_sb_w_o'], 'm_pool_w': out['m_pool_w'], 'm_pool_scale': out['m_pool_scale'], 'm_conv_w_in': out['m_conv_w_in'], 'm_conv_w': out['m_conv_w'], 'm_conv_w_out': out['m_conv_w_out'], 'm_ffn_w_gate': out['m_ffn_w_gate'], 'm_ffn_w_up': out['m_ffn_w_up'], 'm_ffn_w_down': out['m_ffn_w_down'], 'v_norm_mix_g': out['v_norm_mix_g'], 'v_norm_ffn_g': out['v_norm_ffn_g'], 'v_sb_w_qkv': out['v_sb_w_qkv'], 'v_sb_g_q': out['v_sb_g_q'], 'v_sb_g_k': out['v_sb_g_k'], 'v_sb_w_o': out['v_sb_w_o'], 'v_pool_w': out['v_pool_w'], 'v_pool_scale': out['v_pool_scale'], 'v_conv_w_in': out['v_conv_w_in'], 'v_conv_w': out['v_conv_w'], 'v_conv_w_out': out['v_conv_w_out'], 'v_ffn_w_gate': out['v_ffn_w_gate'], 'v_ffn_w_up': out['v_ffn_w_up'], 'v_ffn_w_down': out['v_ffn_w_down']}


def _loss(weights, diff, rest, loss_target):
    with _jax.named_scope("forward"):
        args = {**rest, TWIN_DIFF_INPUT: diff, **{k: w.astype(_WEIGHT_DTYPES[k]) for k, w in weights.items()}}
        y = _forward(args)
    with _jax.named_scope("loss_head"):
        err = _jnp.square(y.astype(_jnp.float32) - loss_target)
        return 0.5 * _jnp.sum(_jnp.mean(err, axis=-1)) if err.ndim else 0.5 * err


def _adamw(w, g, m, v):
    m = ADAM_B1 * m + (1.0 - ADAM_B1) * g
    v = ADAM_B2 * v + (1.0 - ADAM_B2) * _jnp.square(g)
    m_hat = m / (1.0 - ADAM_B1 ** ADAM_STEP)
    v_hat = v / (1.0 - ADAM_B2 ** ADAM_STEP)
    delta = -ADAM_LR * (m_hat / (_jnp.sqrt(v_hat) + ADAM_EPS) + ADAM_WD * w)
    return delta, m, v


def reference(x, norm_mix_g, norm_ffn_g, sb_w_qkv, sb_g_q, sb_g_k, sb_w_o, pool_w, pool_scale, conv_w_in, conv_w, conv_w_out, ffn_w_gate, ffn_w_up, ffn_w_down, loss_target, m_norm_mix_g, m_norm_ffn_g, m_sb_w_qkv, m_sb_g_q, m_sb_g_k, m_sb_w_o, m_pool_w, m_pool_scale, m_conv_w_in, m_conv_w, m_conv_w_out, m_ffn_w_gate, m_ffn_w_up, m_ffn_w_down, v_norm_mix_g, v_norm_ffn_g, v_sb_w_qkv, v_sb_g_q, v_sb_g_k, v_sb_w_o, v_pool_w, v_pool_scale, v_conv_w_in, v_conv_w, v_conv_w_out, v_ffn_w_gate, v_ffn_w_up, v_ffn_w_down):
    given = dict(x=x, norm_mix_g=norm_mix_g, norm_ffn_g=norm_ffn_g, sb_w_qkv=sb_w_qkv, sb_g_q=sb_g_q, sb_g_k=sb_g_k, sb_w_o=sb_w_o, pool_w=pool_w, pool_scale=pool_scale, conv_w_in=conv_w_in, conv_w=conv_w, conv_w_out=conv_w_out, ffn_w_gate=ffn_w_gate, ffn_w_up=ffn_w_up, ffn_w_down=ffn_w_down, loss_target=loss_target, m_norm_mix_g=m_norm_mix_g, m_norm_ffn_g=m_norm_ffn_g, m_sb_w_qkv=m_sb_w_qkv, m_sb_g_q=m_sb_g_q, m_sb_g_k=m_sb_g_k, m_sb_w_o=m_sb_w_o, m_pool_w=m_pool_w, m_pool_scale=m_pool_scale, m_conv_w_in=m_conv_w_in, m_conv_w=m_conv_w, m_conv_w_out=m_conv_w_out, m_ffn_w_gate=m_ffn_w_gate, m_ffn_w_up=m_ffn_w_up, m_ffn_w_down=m_ffn_w_down, v_norm_mix_g=v_norm_mix_g, v_norm_ffn_g=v_norm_ffn_g, v_sb_w_qkv=v_sb_w_qkv, v_sb_g_q=v_sb_g_q, v_sb_g_k=v_sb_g_k, v_sb_w_o=v_sb_w_o, v_pool_w=v_pool_w, v_pool_scale=v_pool_scale, v_conv_w_in=v_conv_w_in, v_conv_w=v_conv_w, v_conv_w_out=v_conv_w_out, v_ffn_w_gate=v_ffn_w_gate, v_ffn_w_up=v_ffn_w_up, v_ffn_w_down=v_ffn_w_down)
    weights = {n: given[n] for n in TWIN_WEIGHTS}
    shared = {n: given[n] for n in SHARED_INPUTS}
    per_example = {n: given[n] for n in ['x']}
    grad_fn = _jax.value_and_grad(_loss, argnums=(0, 1))

    def one_microbatch(ex, loss_target):
        ex = dict(ex)
        diff = ex.pop(TWIN_DIFF_INPUT)
        return grad_fn(weights, diff, {**shared, **ex}, loss_target)

    if N_MICROBATCH == 1:
        loss, (grad_w, grad_x) = one_microbatch(per_example, given["loss_target"])
    else:
        def body(carry, xs):
            loss_sum, grad_sum = carry
            l_k, (gw_k, gx_k) = one_microbatch(xs[0], xs[1])
            with _jax.named_scope("update"):
                return (loss_sum + l_k, _jax.tree.map(_jnp.add, grad_sum, gw_k)), gx_k

        init = (_jnp.zeros((), _jnp.float32), _jax.tree.map(_jnp.zeros_like, weights))
        (loss, grad_w), grad_x = _jax.lax.scan(body, init, (per_example, given["loss_target"]))
    with _jax.named_scope("update"):
        delta_w, new_m, new_v = {}, {}, {}
        for n in TWIN_WEIGHTS:
            delta_w[n], new_m[n], new_v[n] = _adamw(weights[n], grad_w[n], given["m_" + n], given["v_" + n])
    return (loss, grad_x, *[grad_w[n] for n in TWIN_WEIGHTS], *[delta_w[n] for n in TWIN_WEIGHTS],
            *[new_m[n] for n in TWIN_WEIGHTS], *[new_v[n] for n in TWIN_WEIGHTS])
```

```python
import functools

import jax
import jax.numpy as jnp
from jax import lax
from jax.experimental import pallas as pl
from jax.experimental.pallas import tpu as pltpu

F32 = jnp.float32
BF16 = jnp.bfloat16

N_DEV = 8
N_XY = 4
HEAD_DIM = 128
LANES = 128
POOL_WINDOWS = (2, 4, 8, 16)
POOL_HALO = 16
CONV_HALO = 8
EPS = 1e-6
ADAM_LR = 0.001
ADAM_B1 = 0.9
ADAM_B2 = 0.999
ADAM_EPS = 1e-08
ADAM_WD = 0.01
ADAM_STEP = 10

VMEM_CAP_V7X = 56 * 1024 * 1024
VMEM_FLOOR = 32 * 1024 * 1024

NN = (((1,), (0,)), ((), ()))
NT = (((1,), (1,)), ((), ()))
TN = (((0,), (0,)), ((), ()))
MESH = pl.DeviceIdType.MESH
ANY = pl.BlockSpec(memory_space=pl.ANY)


def _dot(a, b, dims):
    return lax.dot_general(a, b, dims, preferred_element_type=F32)


def _params(step_bytes, scratch_bytes=0):
    need = 2 * step_bytes + scratch_bytes + 3 * step_bytes // 2 + (4 << 20)
    return pltpu.CompilerParams(vmem_limit_bytes=int(min(VMEM_CAP_V7X, max(VMEM_FLOOR, need))))


def _nbytes(shape, dtype):
    n = 1
    for s in shape:
        if s is not None:
            n *= s
    return n * jnp.dtype(dtype).itemsize


def _tile(n, target, mult=8):
    t = min(n, target)
    t -= t % mult
    while t > mult and n % t:
        t -= mult
    assert t >= mult and n % t == 0, (n, target, mult)
    return t


def _row_tile(rows, cols, itemsize, budget, mult=16):
    return _tile(rows, max(mult, budget // (cols * itemsize)), mult)


def _rms_fwd(name, x, g_row, out_dtype):
    T, D = x.shape
    tm = _tile(T, 512)

    def body(x_ref, g_ref, o_ref):
        xv = x_ref[...]
        r = lax.rsqrt(jnp.mean(xv * xv, axis=-1, keepdims=True) + EPS)
        o_ref[...] = (xv * r * g_ref[...]).astype(o_ref.dtype)

    return pl.pallas_call(
        body, name=name, grid=(T // tm,),
        in_specs=[pl.BlockSpec((tm, D), lambda i: (i, 0)), pl.BlockSpec((1, D), lambda i: (0, 0))],
        out_specs=pl.BlockSpec((tm, D), lambda i: (i, 0)),
        out_shape=jax.ShapeDtypeStruct((T, D), out_dtype),
        compiler_params=_params(_nbytes((tm, D), F32) * 2),
    )(x, g_row)


def _rms_bwd(name, dh, x, g_row, dres):
    T, D = x.shape
    tm = _tile(T, 256)

    def body(dh_ref, x_ref, g_ref, dres_ref, dx_ref, dx16_ref, dg_ref):
        xv = x_ref[...]
        r = lax.rsqrt(jnp.mean(xv * xv, axis=-1, keepdims=True) + EPS)
        xh = xv * r
        dhv = dh_ref[...]
        dy = dhv * g_ref[...]
        m = jnp.mean(dy * xh, axis=-1, keepdims=True)
        dx = dres_ref[...] + r * (dy - xh * m)
        dx_ref[...] = dx
        dx16_ref[...] = dx.astype(BF16)

        @pl.when(pl.program_id(0) == 0)
        def _():
            dg_ref[...] = jnp.zeros_like(dg_ref)

        dg_ref[...] += jnp.sum(dhv * xh, axis=0, keepdims=True)

    blk = pl.BlockSpec((tm, D), lambda i: (i, 0))
    row = pl.BlockSpec((1, D), lambda i: (0, 0))
    return pl.pallas_call(
        body, name=name, grid=(T // tm,),
        in_specs=[blk, blk, row, blk], out_specs=[blk, blk, row],
        out_shape=[jax.ShapeDtypeStruct((T, D), F32), jax.ShapeDtypeStruct((T, D), BF16),
                   jax.ShapeDtypeStruct((1, D), F32)],
        compiler_params=_params(_nbytes((tm, D), F32) * 5),
    )(dh, x, g_row, dres)


def _loss_head(name, y, target):
    T, D = y.shape
    tm = _tile(T, 256)

    def body(y_ref, t_ref, dy_ref, dy16_ref, acc_ref):
        e = y_ref[...] - t_ref[...]
        d = e * (1.0 / D)
        dy_ref[...] = d
        dy16_ref[...] = d.astype(BF16)
        s = (e * e).reshape(tm // 8, 8, D).sum(axis=0)
        part = s[:, 0:LANES]
        for k in range(1, D // LANES):
            part = part + s[:, k * LANES:(k + 1) * LANES]

        @pl.when(pl.program_id(0) == 0)
        def _():
            acc_ref[...] = jnp.zeros_like(acc_ref)

        acc_ref[...] += part * (0.5 / D)

    blk = pl.BlockSpec((tm, D), lambda i: (i, 0))
    return pl.pallas_call(
        body, name=name, grid=(T // tm,),
        in_specs=[blk, blk], out_specs=[blk, blk, pl.BlockSpec((8, LANES), lambda i: (0, 0))],
        out_shape=[jax.ShapeDtypeStruct((T, D), F32), jax.ShapeDtypeStruct((T, D), BF16),
                   jax.ShapeDtypeStruct((8, LANES), F32)],
        compiler_params=_params(_nbytes((tm, D), F32) * 4),
    )(y, target)


def _mm_cols(name, a, w, out_dtype):
    T, K = a.shape
    nb, _, ns = w.shape
    tm = _tile(T, 1024)

    def body(a_ref, w_ref, o_ref):
        o_ref[...] = _dot(a_ref[...], w_ref[...], NN).astype(o_ref.dtype)

    step = _nbytes((tm, K), BF16) + _nbytes((K, ns), BF16) + _nbytes((tm, ns), out_dtype)
    return pl.pallas_call(
        body, name=name, grid=(nb, T // tm),
        in_specs=[pl.BlockSpec((tm, K), lambda b, i: (i, 0)),
                  pl.BlockSpec((None, K, ns), lambda b, i: (b, 0, 0))],
        out_specs=pl.BlockSpec((None, tm, ns), lambda b, i: (b, i, 0)),
        out_shape=jax.ShapeDtypeStruct((nb, T, ns), out_dtype),
        compiler_params=_params(step),
    )(a, w)


def _ffn_up(name, h, wg, wu):
    T, K = h.shape
    nb, _, fs = wg.shape
    tm = _tile(T, 1024)

    def body(a_ref, wg_ref, wu_ref, g_ref, u_ref, act_ref):
        av = a_ref[...]
        g = _dot(av, wg_ref[...], NN)
        u = _dot(av, wu_ref[...], NN)
        g_ref[...] = g
        u_ref[...] = u
        act_ref[...] = (g * jax.nn.sigmoid(g) * u).astype(BF16)

    wspec = pl.BlockSpec((None, K, fs), lambda b, i: (b, 0, 0))
    ospec = pl.BlockSpec((None, tm, fs), lambda b, i: (b, i, 0))
    osh = lambda dt: jax.ShapeDtypeStruct((nb, T, fs), dt)
    step = _nbytes((tm, K), BF16) + 2 * _nbytes((K, fs), BF16) + _nbytes((tm, fs), BF16) + 2 * _nbytes((tm, fs), F32)
    return pl.pallas_call(
        body, name=name, grid=(nb, T // tm),
        in_specs=[pl.BlockSpec((tm, K), lambda b, i: (i, 0)), wspec, wspec],
        out_specs=[ospec, ospec, ospec], out_shape=[osh(F32), osh(F32), osh(BF16)],
        compiler_params=_params(step),
    )(h, wg, wu)


def _mm_res(name, a, w, res):
    T, K = a.shape
    N = w.shape[1]
    tm, tn = _tile(T, 512), _tile(N, 1024, LANES)

    def body(a_ref, w_ref, r_ref, o_ref):
        o_ref[...] = r_ref[...] + _dot(a_ref[...], w_ref[...], NN)

    step = _nbytes((tm, K), BF16) + _nbytes((K, tn), BF16) + 2 * _nbytes((tm, tn), F32)
    return pl.pallas_call(
        body, name=name, grid=(N // tn, T // tm),
        in_specs=[pl.BlockSpec((tm, K), lambda j, i: (i, 0)), pl.BlockSpec((K, tn), lambda j, i: (0, j)),
                  pl.BlockSpec((tm, tn), lambda j, i: (i, j))],
        out_specs=pl.BlockSpec((tm, tn), lambda j, i: (i, j)),
        out_shape=jax.ShapeDtypeStruct((T, N), F32),
        compiler_params=_params(step),
    )(a, w, res)


def _ffn_down(name, act, wd, res):
    nb, T, fs = act.shape
    N = wd.shape[2]
    tm, tn = _tile(T, 1024), _tile(N, 1024, LANES)

    def body(a_ref, w_ref, r_ref, o_ref, acc_ref):
        b = pl.program_id(2)

        @pl.when(b == 0)
        def _():
            acc_ref[...] = r_ref[...]

        acc_ref[...] += _dot(a_ref[...], w_ref[...], NN)

        @pl.when(b == nb - 1)
        def _():
            o_ref[...] = acc_ref[...]

    step = _nbytes((tm, fs), BF16) + _nbytes((fs, tn), BF16) + 2 * _nbytes((tm, tn), F32)
    return pl.pallas_call(
        body, name=name, grid=(T // tm, N // tn, nb),
        in_specs=[pl.BlockSpec((None, tm, fs), lambda i, j, b: (b, i, 0)),
                  pl.BlockSpec((None, fs, tn), lambda i, j, b: (b, 0, j)),
                  pl.BlockSpec((tm, tn), lambda i, j, b: (i, j))],
        out_specs=pl.BlockSpec((tm, tn), lambda i, j, b: (i, j)),
        out_shape=jax.ShapeDtypeStruct((T, N), F32),
        scratch_shapes=[pltpu.VMEM((tm, tn), F32)],
        compiler_params=_params(step, _nbytes((tm, tn), F32)),
    )(act, wd, res)


def _mm_nt_rows(name, g, w, out_dtype):
    T, N = g.shape
    nb, ks, _ = w.shape
    tm = _tile(T, 1024)

    def body(g_ref, w_ref, o_ref):
        o_ref[...] = _dot(g_ref[...], w_ref[...], NT).astype(o_ref.dtype)

    step = _nbytes((tm, N), BF16) + _nbytes((ks, N), BF16) + _nbytes((tm, ks), out_dtype)
    return pl.pallas_call(
        body, name=name, grid=(nb, T // tm),
        in_specs=[pl.BlockSpec((tm, N), lambda b, i: (i, 0)),
                  pl.BlockSpec((None, ks, N), lambda b, i: (b, 0, 0))],
        out_specs=pl.BlockSpec((tm, ks), lambda b, i: (i, b)),
        out_shape=jax.ShapeDtypeStruct((T, nb * ks), out_dtype),
        compiler_params=_params(step),
    )(g, w)


def _ffn_dact(name, dx16, wd, gate, up):
    T, N = dx16.shape
    nb, fs, _ = wd.shape
    tm = _tile(T, 1024)

    def body(g_ref, w_ref, gate_ref, up_ref, dg_ref, du_ref):
        dact = _dot(g_ref[...], w_ref[...], NT)
        gt = gate_ref[...]
        s = jax.nn.sigmoid(gt)
        silu = gt * s
        dg_ref[...] = (dact * up_ref[...] * (s * (1.0 + gt * (1.0 - s)))).astype(BF16)
        du_ref[...] = (dact * silu).astype(BF16)

    aspec = pl.BlockSpec((None, tm, fs), lambda b, i: (b, i, 0))
    osh = jax.ShapeDtypeStruct((nb, T, fs), BF16)
    step = _nbytes((tm, N), BF16) + _nbytes((fs, N), BF16) + 2 * _nbytes((tm, fs), BF16) + 3 * _nbytes((tm, fs), F32)
    return pl.pallas_call(
        body, name=name, grid=(nb, T // tm),
        in_specs=[pl.BlockSpec((tm, N), lambda b, i: (i, 0)),
                  pl.BlockSpec((None, fs, N), lambda b, i: (b, 0, 0)), aspec, aspec],
        out_specs=[aspec, aspec], out_shape=[osh, osh],
        compiler_params=_params(step),
    )(dx16, wd, gate, up)


def _ffn_dh(name, dgate, dup, wg, wu):
    nb, T, fs = dgate.shape
    D = wg.shape[1]
    tm, tn = _tile(T, 1024), _tile(D, 1024, LANES)

    def body(dg_ref, du_ref, wg_ref, wu_ref, o_ref, acc_ref):
        b = pl.program_id(2)

        @pl.when(b == 0)
        def _():
            acc_ref[...] = jnp.zeros_like(acc_ref)

        acc_ref[...] += _dot(dg_ref[...], wg_ref[...], NT) + _dot(du_ref[...], wu_ref[...], NT)

        @pl.when(b == nb - 1)
        def _():
            o_ref[...] = acc_ref[...]

    aspec = pl.BlockSpec((None, tm, fs), lambda i, j, b: (b, i, 0))
    wspec = pl.BlockSpec((None, tn, fs), lambda i, j, b: (b, j, 0))
    step = 2 * _nbytes((tm, fs), BF16) + 2 * _nbytes((tn, fs), BF16) + _nbytes((tm, tn), F32)
    return pl.pallas_call(
        body, name=name, grid=(T // tm, D // tn, nb),
        in_specs=[aspec, aspec, wspec, wspec],
        out_specs=pl.BlockSpec((tm, tn), lambda i, j, b: (i, j)),
        out_shape=jax.ShapeDtypeStruct((T, D), F32),
        scratch_shapes=[pltpu.VMEM((tm, tn), F32)],
        compiler_params=_params(step, _nbytes((tm, tn), F32)),
    )(dgate, dup, wg, wu)


def _mm_nt_sections(name, g3, w_std):
    ns_, T, Ds = g3.shape
    D = w_std.shape[0]
    tm, tn, tk = _tile(T, 1024), _tile(D, 1024, LANES), _tile(Ds, 1024, LANES)
    nkk = Ds // tk
    nk = ns_ * nkk

    def body(g_ref, w_ref, o_ref, acc_ref):
        r = pl.program_id(2)

        @pl.when(r == 0)
        def _():
            acc_ref[...] = jnp.zeros_like(acc_ref)

        acc_ref[...] += _dot(g_ref[...], w_ref[...], NT)

        @pl.when(r == nk - 1)
        def _():
            o_ref[...] = acc_ref[...]

    step = _nbytes((tm, tk), BF16) + _nbytes((tn, tk), BF16) + _nbytes((tm, tn), F32)
    return pl.pallas_call(
        body, name=name, grid=(T // tm, D // tn, nk),
        in_specs=[pl.BlockSpec((None, tm, tk), lambda i, j, r: (r // nkk, i, r % nkk)),
                  pl.BlockSpec((tn, tk), lambda i, j, r: (j, r))],
        out_specs=pl.BlockSpec((tm, tn), lambda i, j, r: (i, j)),
        out_shape=jax.ShapeDtypeStruct((T, D), F32),
        scratch_shapes=[pltpu.VMEM((tm, tn), F32)],
        compiler_params=_params(step, _nbytes((tm, tn), F32)),
    )(g3, w_std)


def _wgrad(name, grid, a, a_spec, gs, g_spec, out_shape, o_spec, acc_shape):
    n = len(gs)
    nt = grid[-1]

    def body(*refs):
        a_ref, g_refs, o_refs, acc_refs = refs[0], refs[1:1 + n], refs[1 + n:1 + 2 * n], refs[1 + 2 * n:]
        t = pl.program_id(len(grid) - 1)
        av = a_ref[...]
        for g_ref, o_ref, acc_ref in zip(g_refs, o_refs, acc_refs):
            @pl.when(t == 0)
            def _():
                acc_ref[...] = jnp.zeros_like(acc_ref)

            acc_ref[...] += _dot(av, g_ref[...], TN)

            @pl.when(t == nt - 1)
            def _():
                o_ref[...] = acc_ref[...]

    step = _nbytes(a_spec.block_shape, BF16) + n * (_nbytes(g_spec.block_shape, BF16) + _nbytes(acc_shape, F32))
    outs = pl.pallas_call(
        body, name=name, grid=grid,
        in_specs=[a_spec] + [g_spec] * n, out_specs=[o_spec] * n,
        out_shape=[jax.ShapeDtypeStruct(out_shape, F32)] * n,
        scratch_shapes=[pltpu.VMEM(acc_shape, F32)] * n,
        compiler_params=_params(step, n * _nbytes(acc_shape, F32)),
    )(a, *gs)
    return outs


def _wgrad_cols(name, a, gs):
    T, K = a.shape
    nb, _, ns = gs[0].shape
    tk, tt = _tile(K, 1024, LANES), _tile(T, 1024)
    return _wgrad(name, (nb, K // tk, T // tt), a,
                  pl.BlockSpec((tt, tk), lambda b, k, t: (t, k)), gs,
                  pl.BlockSpec((None, tt, ns), lambda b, k, t: (b, t, 0)),
                  (nb, K, ns), pl.BlockSpec((None, tk, ns), lambda b, k, t: (b, k, 0)), (tk, ns))


def _wgrad_rows(name, act, dx16):
    nb, T, fs = act.shape
    N = dx16.shape[1]
    tn, tt = _tile(N, 1024, LANES), _tile(T, 1024)
    return _wgrad(name, (nb, N // tn, T // tt), act,
                  pl.BlockSpec((None, tt, fs), lambda b, j, t: (b, t, 0)), [dx16],
                  pl.BlockSpec((tt, tn), lambda b, j, t: (t, j)),
                  (nb, fs, N), pl.BlockSpec((None, fs, tn), lambda b, j, t: (b, 0, j)), (fs, tn))[0]


def _wgrad_std(name, a, g):
    T, K = a.shape
    N = g.shape[1]
    tk, tn, tt = _tile(K, 1024, LANES), _tile(N, 1024, LANES), _tile(T, 1024)
    return _wgrad(name, (K // tk, N // tn, T // tt), a,
                  pl.BlockSpec((tt, tk), lambda k, j, t: (t, k)), [g],
                  pl.BlockSpec((tt, tn), lambda k, j, t: (t, j)),
                  (K, N), pl.BlockSpec((tk, tn), lambda k, j, t: (k, j)), (tk, tn))[0]


def _wgrad_sections(name, a, g3, ns):
    T, K = a.shape
    nsec, _, Ds = g3.shape
    cw = 256 if (ns % 256 == 0 and Ds % 256 == 0) else LANES
    per_sec, per_shard = Ds // cw, ns // cw
    nb = nsec * Ds // ns
    tk, tt = _tile(K, 1024, LANES), _tile(T, 1024)
    return _wgrad(name, (nsec * per_sec, K // tk, T // tt), a,
                  pl.BlockSpec((tt, tk), lambda p, k, t: (t, k)), [g3],
                  pl.BlockSpec((None, tt, cw), lambda p, k, t: (p // per_sec, t, p % per_sec)),
                  (nb, K, ns), pl.BlockSpec((None, tk, cw), lambda p, k, t: (p // per_shard, k, p % per_shard)),
                  (tk, cw))[0]


def _tri2():
    r = lax.broadcasted_iota(jnp.int32, (LANES, 2 * LANES), 0)
    c = lax.broadcasted_iota(jnp.int32, (LANES, 2 * LANES), 1)
    return jnp.where((r >= c) | (c >= LANES), 1.0, 0.0).astype(BF16)


def _suffix_sums(v, tri):
    hi = v.astype(BF16)
    lo = (v - hi.astype(F32)).astype(BF16)
    both = _dot(hi, tri, NN) + _dot(lo, tri, NN)
    return both[:, :LANES], both[:, LANES:]


def _sb_block(z, mask, carry, tri):
    lsn = -(jnp.maximum(z, 0.0) + jnp.log(1.0 + jnp.exp(-jnp.abs(z))))
    lk = jnp.where(mask, lsn, 0.0)
    incl, tot = _suffix_sums(lk, tri)
    a = jnp.where(mask, jnp.exp(z + lsn + (incl - lk + carry)), 0.0)
    return a, lsn, tot


def _head_norm_store(src_ref, g_ref, dst_ref, rows, chunk):
    def step(i, _):
        r0 = pl.multiple_of(i * chunk, chunk)
        v = src_ref[pl.ds(r0, chunk), :]
        r = lax.rsqrt(jnp.mean(v * v, axis=-1, keepdims=True) + EPS)
        dst_ref[pl.ds(r0, chunk), :] = (v * r * g_ref[...]).astype(dst_ref.dtype)
        return 0

    lax.fori_loop(0, rows // chunk, step, 0)


def _qkv_specs(T, ns, H):
    cps = ns // HEAD_DIM

    def spec(sec):
        return pl.BlockSpec((None, T, HEAD_DIM), lambda h: ((sec * H + h) // cps, 0, (sec * H + h) % cps))

    return [spec(0), spec(1), spec(2)]


def _attn_fwd(name, qkv, gq, gk, H):
    _, T, ns = qkv.shape
    D = H * HEAD_DIM
    BQ = _tile(T, 256, LANES)
    kpb = BQ // LANES
    scale = HEAD_DIM ** -0.5

    def body(q_ref, k_ref, v_ref, gq_ref, gk_ref, o32_ref, o16_ref, qn, kn, vb):
        _head_norm_store(q_ref, gq_ref, qn, T, BQ)
        _head_norm_store(k_ref, gk_ref, kn, T, BQ)
        vb[...] = v_ref[...].astype(BF16)
        tri = _tri2()
        row = lax.broadcasted_iota(jnp.int32, (BQ, LANES), 0)
        col = lax.broadcasted_iota(jnp.int32, (BQ, LANES), 1)

        def qloop(qi, _):
            t0 = pl.multiple_of(qi * BQ, BQ)
            qb = qn[pl.ds(t0, BQ), :]
            nkb = (qi + 1) * kpb
            t_idx = t0 + row

            def kloop(jj, carry):
                o_acc, o_low, cr = carry
                s0 = pl.multiple_of((nkb - 1 - jj) * LANES, LANES)
                z = _dot(qb, kn[pl.ds(s0, LANES), :], NT) * scale
                a, _, tot = _sb_block(z, (s0 + col) < t_idx, cr, tri)
                vj = vb[pl.ds(s0, LANES), :]
                a_hi = a.astype(BF16)
                o_acc = o_acc + _dot(a_hi, vj, NN)
                o_low = o_low + _dot((a - a_hi.astype(F32)).astype(BF16), vj, NN)
                return o_acc, o_low, cr + tot

            zero = jnp.zeros((BQ, LANES), F32)
            o_acc, o_low, _ = lax.fori_loop(0, nkb, kloop, (zero, zero, zero))
            o32_ref[pl.ds(t0, BQ), :] = o_acc + o_low
            o16_ref[pl.ds(t0, BQ), :] = o_acc.astype(BF16)
            return 0

        lax.fori_loop(0, T // BQ, qloop, 0)

    gspec = pl.BlockSpec((1, HEAD_DIM), lambda h: (0, 0))
    ospec = pl.BlockSpec((T, HEAD_DIM), lambda h: (0, h))
    step = 3 * _nbytes((T, HEAD_DIM), F32) + _nbytes((T, HEAD_DIM), F32) + _nbytes((T, HEAD_DIM), BF16)
    return pl.pallas_call(
        body, name=name, grid=(H,),
        in_specs=_qkv_specs(T, ns, H) + [gspec, gspec],
        out_specs=[ospec, ospec],
        out_shape=[jax.ShapeDtypeStruct((T, D), F32), jax.ShapeDtypeStruct((T, D), BF16)],
        scratch_shapes=[pltpu.VMEM((T, HEAD_DIM), BF16)] * 3,
        compiler_params=_params(step, 3 * _nbytes((T, HEAD_DIM), BF16)),
    )(qkv, qkv, qkv, gq, gk)


def _attn_bwd(name, qkv, gq, gk, do16, o32, H):
    _, T, ns = qkv.shape
    D = H * HEAD_DIM
    BQ = _tile(T, 256, LANES)
    kpb = BQ // LANES
    scale = HEAD_DIM ** -0.5

    def body(q_ref, k_ref, v_ref, gq_ref, gk_ref, do_ref, o_ref, d3_ref, dgq_ref, dgk_ref,
             qn, kn, vb, dqn, dkn, dv):
        h = pl.program_id(0)
        _head_norm_store(q_ref, gq_ref, qn, T, BQ)
        _head_norm_store(k_ref, gk_ref, kn, T, BQ)
        vb[...] = v_ref[...].astype(BF16)
        dkn[...] = jnp.zeros_like(dkn)
        dv[...] = jnp.zeros_like(dv)
        tri = _tri2()
        row = lax.broadcasted_iota(jnp.int32, (BQ, LANES), 0)
        col = lax.broadcasted_iota(jnp.int32, (BQ, LANES), 1)

        def qloop(qi, _):
            t0 = pl.multiple_of(qi * BQ, BQ)
            qb = qn[pl.ds(t0, BQ), :]
            dob = do_ref[pl.ds(t0, BQ), :]
            total = jnp.sum(dob.astype(F32) * o_ref[pl.ds(t0, BQ), :], axis=-1, keepdims=True)
            nkb = (qi + 1) * kpb
            t_idx = t0 + row

            def kloop(jj, carry):
                dq_acc, cr, crd = carry
                s0 = pl.multiple_of((nkb - 1 - jj) * LANES, LANES)
                kb = kn[pl.ds(s0, LANES), :]
                vj = vb[pl.ds(s0, LANES), :]
                z = _dot(qb, kb, NT) * scale
                mask = (s0 + col) < t_idx
                a, lsn, tot = _sb_block(z, mask, cr, tri)
                dla = a * _dot(dob, vj, NT)
                incl_d, tot_d = _suffix_sums(dla, tri)
                sig = jnp.exp(z + lsn)
                d_keep = total - (incl_d + crd)
                dz = (dla * (1.0 - sig) - jnp.where(mask, d_keep * sig, 0.0)) * scale
                dzb = dz.astype(BF16)
                dq_acc = dq_acc + _dot(dzb, kb, NN)
                dkn[pl.ds(s0, LANES), :] += _dot(dzb, qb, TN)
                dv[pl.ds(s0, LANES), :] += _dot(a.astype(BF16), dob, TN)
                return dq_acc, cr + tot, crd + tot_d

            zero = jnp.zeros((BQ, LANES), F32)
            dq_acc, _, _ = lax.fori_loop(0, nkb, kloop, (zero, zero, zero))
            dqn[pl.ds(t0, BQ), :] = dq_acc
            return 0

        lax.fori_loop(0, T // BQ, qloop, 0)

        @pl.when(h == 0)
        def _():
            dgq_ref[...] = jnp.zeros_like(dgq_ref)
            dgk_ref[...] = jnp.zeros_like(dgk_ref)

        def norm_bwd(src_ref, g_ref, dy_ref, sec, dg_ref):
            def step(i, _):
                r0 = pl.multiple_of(i * BQ, BQ)
                v = src_ref[pl.ds(r0, BQ), :]
                r = lax.rsqrt(jnp.mean(v * v, axis=-1, keepdims=True) + EPS)
                vh = v * r
                dyo = dy_ref[pl.ds(r0, BQ), :]
                dy = dyo * g_ref[...]
                m = jnp.mean(dy * vh, axis=-1, keepdims=True)
                d3_ref[sec, pl.ds(r0, BQ), :] = (r * (dy - vh * m)).astype(BF16)
                dg_ref[...] += jnp.sum(dyo * vh, axis=0, keepdims=True)
                return 0

            lax.fori_loop(0, T // BQ, step, 0)

        norm_bwd(q_ref, gq_ref, dqn, 0, dgq_ref)
        norm_bwd(k_ref, gk_ref, dkn, 1, dgk_ref)
        d3_ref[2] = dv[...].astype(BF16)

    gspec = pl.BlockSpec((1, HEAD_DIM), lambda h: (0, 0))
    hspec = pl.BlockSpec((T, HEAD_DIM), lambda h: (0, h))
    step = (3 * _nbytes((T, HEAD_DIM), F32) + _nbytes((T, HEAD_DIM), BF16) + _nbytes((T, HEAD_DIM), F32)
            + 3 * _nbytes((T, HEAD_DIM), BF16))
    scratch = 3 * _nbytes((T, HEAD_DIM), BF16) + 3 * _nbytes((T, HEAD_DIM), F32)
    return pl.pallas_call(
        body, name=name, grid=(H,),
        in_specs=_qkv_specs(T, ns, H) + [gspec, gspec, hspec, hspec],
        out_specs=[pl.BlockSpec((3, T, HEAD_DIM), lambda h: (0, 0, h)), gspec, gspec],
        out_shape=[jax.ShapeDtypeStruct((3, T, D), BF16), jax.ShapeDtypeStruct((1, HEAD_DIM), F32),
                   jax.ShapeDtypeStruct((1, HEAD_DIM), F32)],
        scratch_shapes=[pltpu.VMEM((T, HEAD_DIM), BF16)] * 3 + [pltpu.VMEM((T, HEAD_DIM), F32)] * 3,
        compiler_params=_params(step, scratch),
    )(qkv, qkv, qkv, gq, gk, do16, o32)


def _by_group(g, vals):
    out = vals[-1]
    for k in range(len(vals) - 2, -1, -1):
        out = jnp.where(g == k, vals[k], out)
    return out


def _pool_fwd(name, hf, x, wp, scale_row):
    T, D = x.shape
    G = len(POOL_WINDOWS)
    C = D // G
    tm = _tile(T, 512, POOL_HALO)
    hb = tm // POOL_HALO

    def body(h_ref, halo_ref, x_ref, w_ref, s_ref, xo_ref, p_ref):
        g, i = pl.program_id(0), pl.program_id(1)
        hv = h_ref[...]
        ext = jnp.concatenate([halo_ref[...] * (i > 0).astype(F32), hv], axis=0)
        sums, acc = [], ext
        for k in range(len(POOL_WINDOWS)):
            acc = acc + pltpu.roll(acc, 1 << k, 0)
            sums.append(acc)
        ws = _by_group(g, sums)[POOL_HALO:]
        t = i * tm + lax.broadcasted_iota(jnp.int32, (tm, 1), 0)
        cnt = jnp.minimum(t + 1, lax.shift_left(jnp.int32(2), g)).astype(F32)
        p = (ws / cnt - hv).astype(BF16)
        p_ref[...] = p
        xo_ref[...] = x_ref[...] + _dot(p, w_ref[...], NN) * s_ref[...]

    blk = pl.BlockSpec((tm, C), lambda g, i: (i, g))
    step = 3 * _nbytes((tm, C), F32) + _nbytes((tm, C), BF16) + _nbytes((C, C), BF16)
    return pl.pallas_call(
        body, name=name, grid=(G, T // tm),
        in_specs=[blk, pl.BlockSpec((POOL_HALO, C), lambda g, i: (jnp.maximum(i * hb - 1, 0), g)), blk,
                  pl.BlockSpec((None, C, C), lambda g, i: (g, 0, 0)), pl.BlockSpec((1, C), lambda g, i: (0, g))],
        out_specs=[blk, blk],
        out_shape=[jax.ShapeDtypeStruct((T, D), F32), jax.ShapeDtypeStruct((T, D), BF16)],
        compiler_params=_params(step + 6 * _nbytes((tm, C), F32)),
    )(hf, hf, x, wp, scale_row)


def _pool_bwd(name, dx, p, wp, scale_row):
    T, D = dx.shape
    G = len(POOL_WINDOWS)
    C = D // G
    tm = _tile(T, 512, POOL_HALO)
    hb = tm // POOL_HALO
    nt = T // tm
    n = tm + POOL_HALO

    def body(dx_ref, halo_ref, p_ref, w_ref, s_ref, dh_ref, dw_ref, ds_ref):
        g, i = pl.program_id(0), pl.program_id(1)
        dxv = dx_ref[...]
        dxe = jnp.concatenate([dxv, halo_ref[...] * (i < nt - 1).astype(F32)], axis=0)
        dyp = (dxe * s_ref[...]).astype(BF16)
        wv = w_ref[...]
        dp = _dot(dyp, wv, NT)
        t = i * tm + lax.broadcasted_iota(jnp.int32, (n, 1), 0)
        cnt = jnp.minimum(t + 1, lax.shift_left(jnp.int32(2), g)).astype(F32)
        sums, acc = [], dp / cnt
        for k in range(len(POOL_WINDOWS)):
            acc = acc + pltpu.roll(acc, n - (1 << k), 0)
            sums.append(acc)
        dh_ref[...] = _by_group(g, sums)[:tm] - dp[:tm]
        pv = p_ref[...]

        @pl.when(i == 0)
        def _():
            dw_ref[...] = jnp.zeros_like(dw_ref)
            ds_ref[...] = jnp.zeros_like(ds_ref)

        ds_ref[...] += jnp.sum(dxv * _dot(pv, wv, NN), axis=0, keepdims=True)
        dw_ref[...] += _dot(pv, dyp[:tm], TN)

    blk = pl.BlockSpec((tm, C), lambda g, i: (i, g))
    step = 2 * _nbytes((tm, C), F32) + _nbytes((tm, C), BF16) + _nbytes((C, C), BF16) + _nbytes((C, C), F32)
    return pl.pallas_call(
        body, name=name, grid=(G, nt),
        in_specs=[blk, pl.BlockSpec((POOL_HALO, C), lambda g, i: (jnp.minimum((i + 1) * hb, T // POOL_HALO - 1), g)),
                  blk, pl.BlockSpec((None, C, C), lambda g, i: (g, 0, 0)), pl.BlockSpec((1, C), lambda g, i: (0, g))],
        out_specs=[blk, pl.BlockSpec((None, C, C), lambda g, i: (g, 0, 0)), pl.BlockSpec((1, C), lambda g, i: (0, g))],
        out_shape=[jax.ShapeDtypeStruct((T, D), F32), jax.ShapeDtypeStruct((G, C, C), F32),
                   jax.ShapeDtypeStruct((1, D), F32)],
        compiler_params=_params(step + 8 * _nbytes((tm, C), F32)),
    )(dx, dx, p, wp, scale_row)


def _section_spec(rows, cw, ns, D, sec, row_map):
    per = ns // cw

    def imap(j, i):
        c = (sec * D) // cw + j
        return (c // per, row_map(i), c % per)

    return pl.BlockSpec((None, rows, cw), imap)


def _conv_fwd(name, bcx, cw_full):
    _, T, ns = bcx.shape
    D = cw_full.shape[1]
    cw = 256 if (ns % 256 == 0 and D % 256 == 0) else LANES
    tm = _tile(T, 512, CONV_HALO)
    hb = tm // CONV_HALO

    def body(b_ref, c_ref, u_ref, ch_ref, uh_ref, w_ref, o_ref):
        i = pl.program_id(1)
        gm = c_ref[...] * u_ref[...]
        ext = jnp.concatenate([ch_ref[...] * uh_ref[...] * (i > 0).astype(F32), gm], axis=0)
        w0, w1, w2 = w_ref[0:1, :], w_ref[1:2, :], w_ref[2:3, :]
        y = w2 * gm + w1 * pltpu.roll(ext, 1, 0)[CONV_HALO:] + w0 * pltpu.roll(ext, 2, 0)[CONV_HALO:]
        o_ref[...] = (b_ref[...] * y).astype(BF16)

    main = lambda i: i
    prev = lambda i: jnp.maximum(i * hb - 1, 0)
    return pl.pallas_call(
        body, name=name, grid=(D // cw, T // tm),
        in_specs=[_section_spec(tm, cw, ns, D, 0, main), _section_spec(tm, cw, ns, D, 1, main),
                  _section_spec(tm, cw, ns, D, 2, main), _section_spec(CONV_HALO, cw, ns, D, 1, prev),
                  _section_spec(CONV_HALO, cw, ns, D, 2, prev), pl.BlockSpec((3, cw), lambda j, i: (0, j))],
        out_specs=pl.BlockSpec((tm, cw), lambda j, i: (i, j)),
        out_shape=jax.ShapeDtypeStruct((T, D), BF16),
        compiler_params=_params(8 * _nbytes((tm, cw), F32)),
    )(bcx, bcx, bcx, bcx, bcx, cw_full)


def _conv_bwd(name, dby, bcx, cw_full):
    _, T, ns = bcx.shape
    D = cw_full.shape[1]
    cw = 256 if (ns % 256 == 0 and D % 256 == 0) else LANES
    tm = _tile(T, 512, CONV_HALO)
    hb = tm // CONV_HALO
    nt = T // tm
    n = tm + CONV_HALO

    def body(dby_ref, dbyh_ref, b_ref, bh_ref, c_ref, u_ref, ch_ref, uh_ref, w_ref, d3_ref, dw_ref):
        i = pl.program_id(1)
        w0, w1, w2 = w_ref[0:1, :], w_ref[1:2, :], w_ref[2:3, :]
        bv, cv, uv, dbyv = b_ref[...], c_ref[...], u_ref[...], dby_ref[...]
        gm = cv * uv
        ext_g = jnp.concatenate([ch_ref[...] * uh_ref[...] * (i > 0).astype(F32), gm], axis=0)
        g1 = pltpu.roll(ext_g, 1, 0)[CONV_HALO:]
        g2 = pltpu.roll(ext_g, 2, 0)[CONV_HALO:]
        y = w2 * gm + w1 * g1 + w0 * g2
        dy = dbyv * bv
        ext_dy = jnp.concatenate([dy, dbyh_ref[...] * bh_ref[...] * (i < nt - 1).astype(F32)], axis=0)
        dg = w2 * dy + w1 * pltpu.roll(ext_dy, n - 1, 0)[:tm] + w0 * pltpu.roll(ext_dy, n - 2, 0)[:tm]
        d3_ref[0] = (dbyv * y).astype(BF16)
        d3_ref[1] = (dg * uv).astype(BF16)
        d3_ref[2] = (dg * cv).astype(BF16)

        @pl.when(i == 0)
        def _():
            dw_ref[...] = jnp.zeros_like(dw_ref)

        rows = [jnp.sum(dy * v, axis=0, keepdims=True) for v in (g2, g1, gm)]
        dw_ref[...] += jnp.concatenate(rows + [jnp.zeros((8 - len(rows), cw), F32)], axis=0)

    main = lambda i: i
    prev = lambda i: jnp.maximum(i * hb - 1, 0)
    nxt = lambda i: jnp.minimum((i + 1) * hb, T // CONV_HALO - 1)
    return pl.pallas_call(
        body, name=name, grid=(D // cw, nt),
        in_specs=[pl.BlockSpec((tm, cw), lambda j, i: (i, j)), pl.BlockSpec((CONV_HALO, cw), lambda j, i: (nxt(i), j)),
                  _section_spec(tm, cw, ns, D, 0, main), _section_spec(CONV_HALO, cw, ns, D, 0, nxt),
                  _section_spec(tm, cw, ns, D, 1, main), _section_spec(tm, cw, ns, D, 2, main),
                  _section_spec(CONV_HALO, cw, ns, D, 1, prev), _section_spec(CONV_HALO, cw, ns, D, 2, prev),
                  pl.BlockSpec((3, cw), lambda j, i: (0, j))],
        out_specs=[pl.BlockSpec((3, tm, cw), lambda j, i: (0, i, j)), pl.BlockSpec((8, cw), lambda j, i: (0, j))],
        out_shape=[jax.ShapeDtypeStruct((3, T, D), BF16), jax.ShapeDtypeStruct((8, D), F32)],
        compiler_params=_params(14 * _nbytes((tm, cw), F32)),
    )(dby, dby, bcx, bcx, bcx, bcx, bcx, bcx, cw_full)


def _position():
    return lax.axis_index("x"), lax.axis_index("y"), lax.axis_index("c")


def _all_gather(name, tensors):
    n = len(tensors)
    shapes = [a.shape[1:] if idx is not None else a.shape for a, idx in tensors]

    def body(*refs):
        srcs, outs = refs[:n], refs[n:2 * n]
        send_sems, recv_sems, local_sems = refs[2 * n:]
        x, y, c = _position()
        me, sibling = (x, y, c), (x, y, 1 - c)
        chips = [(1 - x, y), (x, 1 - y), (1 - x, 1 - y)]

        def slot(out, p):
            return out.at[4 * p[0] + 2 * p[1] + p[2]]

        def copy(t, k, block, to, src=None):
            return pltpu.make_async_remote_copy(
                src_ref=slot(outs[t], block) if src is None else src, dst_ref=slot(outs[t], block),
                send_sem=send_sems.at[7 * t + k], recv_sem=recv_sems.at[7 * t + k],
                device_id=to, device_id_type=MESH)

        started, mine = [], []
        for t, (_, idx) in enumerate(tensors):
            src = srcs[t] if idx is None else srcs[t].at[idx]
            own = pltpu.make_async_copy(src, slot(outs[t], me), local_sems.at[t])
            own.start()
            mine.append(own)
            first = [copy(t, 0, me, sibling, src=src)]
            first += [copy(t, 1 + j, me, (*chip, c), src=src) for j, chip in enumerate(chips)]
            for cp in first:
                cp.start()
            started += first
        for t in range(n):
            for j, chip in enumerate(chips):
                copy(t, 1 + j, (*chip, c), me).wait_recv()
                passed = copy(t, 4 + j, (*chip, c), sibling)
                passed.start()
                started.append(passed)
        for t in range(n):
            copy(t, 0, sibling, me).wait_recv()
            for j, chip in enumerate(chips):
                copy(t, 4 + j, (*chip, 1 - c), me).wait_recv()
        for cp in started:
            cp.wait_send()
        for own in mine:
            own.wait()

    return pl.pallas_call(
        body, name=name,
        in_specs=[ANY] * n, out_specs=[ANY] * n,
        out_shape=[jax.ShapeDtypeStruct((N_DEV,) + tuple(s), a.dtype) for s, (a, _) in zip(shapes, tensors)],
        scratch_shapes=[pltpu.SemaphoreType.DMA((7 * n,)), pltpu.SemaphoreType.DMA((7 * n,)),
                        pltpu.SemaphoreType.DMA((n,))],
    )(*[a for a, _ in tensors])


def _swap_halves(name, grads):
    n = len(grads)

    def body(*refs):
        srcs, got, kept = refs[:n], refs[n:2 * n], refs[2 * n:3 * n]
        send_sems, recv_sems, local_sems = refs[3 * n:]
        x, y, c = _position()
        sends, keeps = [], []
        for t in range(n):
            for k in range(N_XY):
                cp = pltpu.make_async_remote_copy(
                    src_ref=srcs[t].at[2 * k + 1 - c], dst_ref=got[t].at[k],
                    send_sem=send_sems.at[N_XY * t + k], recv_sem=recv_sems.at[N_XY * t + k],
                    device_id=(x, y, 1 - c), device_id_type=MESH)
                cp.start()
                sends.append(cp)
                kp = pltpu.make_async_copy(srcs[t].at[2 * k + c], kept[t].at[k], local_sems.at[N_XY * t + k])
                kp.start()
                keeps.append(kp)
        for cp in sends:
            cp.wait()
        for kp in keeps:
            kp.wait()

    half = [jax.ShapeDtypeStruct((N_XY,) + g.shape[1:], g.dtype) for g in grads]
    outs = pl.pallas_call(
        body, name=name, in_specs=[ANY] * n, out_specs=[ANY] * (2 * n), out_shape=half + half,
        scratch_shapes=[pltpu.SemaphoreType.DMA((N_XY * n,))] * 3,
    )(*grads)
    return outs[:n], outs[n:]


def _add_to_bf16(name, a, b):
    nk, R, C = a.shape
    tr = _row_tile(R, C, 4, 2 << 20)

    def body(a_ref, b_ref, o_ref):
        o_ref[...] = (a_ref[...] + b_ref[...]).astype(BF16)

    blk = pl.BlockSpec((None, tr, C), lambda k, i: (k, i, 0))
    return pl.pallas_call(
        body, name=name, grid=(nk, R // tr), in_specs=[blk, blk], out_specs=blk,
        out_shape=jax.ShapeDtypeStruct(a.shape, BF16),
        compiler_params=_params(3 * _nbytes((tr, C), F32)),
    )(a, b)


def _swap_chips(name, partials):
    n = len(partials)

    def body(*refs):
        srcs, got = refs[:n], refs[n:2 * n]
        send_sems, recv_sems, local_sems = refs[2 * n:]
        x, y, c = _position()
        mine = 2 * x + y
        sends, keeps = [], []
        for t in range(n):
            kp = pltpu.make_async_copy(srcs[t].at[mine], got[t].at[mine], local_sems.at[t])
            kp.start()
            keeps.append(kp)
            for d in range(1, N_XY):
                px, py = x ^ (d >> 1), y ^ (d & 1)
                cp = pltpu.make_async_remote_copy(
                    src_ref=srcs[t].at[2 * px + py], dst_ref=got[t].at[mine],
                    send_sem=send_sems.at[3 * t + d - 1], recv_sem=recv_sems.at[3 * t + d - 1],
                    device_id=(px, py, c), device_id_type=MESH)
                cp.start()
                sends.append(cp)
        for cp in sends:
            cp.wait()
        for kp in keeps:
            kp.wait()

    return pl.pallas_call(
        body, name=name, in_specs=[ANY] * n, out_specs=[ANY] * n,
        out_shape=[jax.ShapeDtypeStruct(p.shape, p.dtype) for p in partials],
        scratch_shapes=[pltpu.SemaphoreType.DMA((3 * n,)), pltpu.SemaphoreType.DMA((3 * n,)),
                        pltpu.SemaphoreType.DMA((n,))],
    )(*partials)


def _adamw(w, g, m, v):
    m = ADAM_B1 * m + (1.0 - ADAM_B1) * g
    v = ADAM_B2 * v + (1.0 - ADAM_B2) * (g * g)
    m_hat = m / (1.0 - ADAM_B1 ** ADAM_STEP)
    v_hat = v / (1.0 - ADAM_B2 ** ADAM_STEP)
    return -ADAM_LR * (m_hat / (jnp.sqrt(v_hat) + ADAM_EPS) + ADAM_WD * w), m, v


def _reduce_update(name, parts, w, m, v, layer, prev):
    _, R, C = parts.shape
    L = w.shape[0]
    tr = _row_tile(R, C, 4, 1 << 20, 8)

    def body(p_ref, w_ref, m_ref, v_ref, *rest):
        g_ref, d_ref, mo_ref, vo_ref = rest[-4:]
        g = p_ref[0].astype(F32)
        for k in range(1, N_XY):
            g = g + p_ref[k].astype(F32)
        g_ref[...] = g
        d_ref[...], mo_ref[...], vo_ref[...] = _adamw(w_ref[...], g, m_ref[...], v_ref[...])

    lay = pl.BlockSpec((None, tr, C), lambda i: (layer, i, 0))
    osh = jax.ShapeDtypeStruct((L, R, C), F32)
    extra = [] if prev is None else list(prev)
    return pl.pallas_call(
        body, name=name, grid=(R // tr,),
        in_specs=[pl.BlockSpec((N_XY, tr, C), lambda i: (0, i, 0)), lay, lay, lay] + [ANY] * len(extra),
        out_specs=[lay] * 4, out_shape=[osh] * 4,
        input_output_aliases={4 + k: k for k in range(len(extra))},
        compiler_params=_params(9 * _nbytes((tr, C), F32)),
    )(parts, w, m, v, *extra)


def _all_reduce_small(name, vec):
    R = vec.shape[0]

    def body(v_ref, o_ref, tot_ref, buf, send_sems, recv_sems):
        x, y, c = _position()
        me = 4 * x + 2 * y + c
        buf[me] = v_ref[...]
        sends = []
        for d in range(1, N_DEV):
            cp = pltpu.make_async_remote_copy(
                src_ref=v_ref, dst_ref=buf.at[me], send_sem=send_sems.at[d - 1], recv_sem=recv_sems.at[d - 1],
                device_id=(x ^ (d >> 2), y ^ ((d >> 1) & 1), c ^ (d & 1)), device_id_type=MESH)
            cp.start()
            sends.append(cp)
        for cp in sends:
            cp.wait()
        s = buf[0]
        for k in range(1, N_DEV):
            s = s + buf[k]
        o_ref[...] = s
        tot_ref[...] = jnp.sum(jnp.sum(s[0:8], axis=0, keepdims=True), axis=1, keepdims=True)

    vm = pl.BlockSpec(memory_space=pltpu.VMEM)
    return pl.pallas_call(
        body, name=name, in_specs=[vm], out_specs=[vm, vm],
        out_shape=[jax.ShapeDtypeStruct(vec.shape, F32), jax.ShapeDtypeStruct((1, 1), F32)],
        scratch_shapes=[pltpu.VMEM((N_DEV, R, LANES), F32), pltpu.SemaphoreType.DMA((N_DEV - 1,)),
                        pltpu.SemaphoreType.DMA((N_DEV - 1,))],
    )(vec)


def _adamw_small(name, w, g, m, v):
    def body(w_ref, g_ref, m_ref, v_ref, d_ref, mo_ref, vo_ref):
        d_ref[...], mo_ref[...], vo_ref[...] = _adamw(w_ref[...], g_ref[...], m_ref[...], v_ref[...])

    vm = pl.BlockSpec(memory_space=pltpu.VMEM)
    return pl.pallas_call(
        body, name=name, in_specs=[vm] * 4, out_specs=[vm] * 3,
        out_shape=[jax.ShapeDtypeStruct(w.shape, F32)] * 3,
    )(w, g, m, v)


def _pack(parts, rows):
    flat = jnp.concatenate([p.reshape(-1) for p in parts])
    return jnp.pad(flat, (0, rows * LANES - flat.shape[0])).reshape(rows, LANES)


def _unpack(packed, shapes, skip=0):
    flat, out, off = packed.reshape(-1), [], skip
    for s in shapes:
        n = 1
        for d in s:
            n *= d
        out.append(flat[off:off + n].reshape(s))
        off += n
    return out


def kernel(x, norm_mix_g, norm_ffn_g, sb_w_qkv, sb_g_q, sb_g_k, sb_w_o, pool_w, pool_scale, conv_w_in, conv_w, conv_w_out, ffn_w_gate, ffn_w_up, ffn_w_down, loss_target, m_norm_mix_g, m_norm_ffn_g, m_sb_w_qkv, m_sb_g_q, m_sb_g_k, m_sb_w_o, m_pool_w, m_pool_scale, m_conv_w_in, m_conv_w, m_conv_w_out, m_ffn_w_gate, m_ffn_w_up, m_ffn_w_down, v_norm_mix_g, v_norm_ffn_g, v_sb_w_qkv, v_sb_g_q, v_sb_g_k, v_sb_w_o, v_pool_w, v_pool_scale, v_conv_w_in, v_conv_w, v_conv_w_out, v_ffn_w_gate, v_ffn_w_up, v_ffn_w_down):
    _, T, D = x.shape
    depth = norm_mix_g.shape[0]
    H = D // HEAD_DIM
    G = len(POOL_WINDOWS)
    pool_rows = pool_w.shape[2]
    xs = x[0]

    big = {
        "sb_w_qkv": (sb_w_qkv, m_sb_w_qkv, v_sb_w_qkv), "sb_w_o": (sb_w_o, m_sb_w_o, v_sb_w_o),
        "pool_w": tuple(a.reshape(a.shape[0], G * pool_rows, a.shape[3]) for a in (pool_w, m_pool_w, v_pool_w)),
        "conv_w_in": (conv_w_in, m_conv_w_in, v_conv_w_in), "conv_w_out": (conv_w_out, m_conv_w_out, v_conv_w_out),
        "ffn_w_gate": (ffn_w_gate, m_ffn_w_gate, v_ffn_w_gate), "ffn_w_up": (ffn_w_up, m_ffn_w_up, v_ffn_w_up),
        "ffn_w_down": (ffn_w_down, m_ffn_w_down, v_ffn_w_down),
    }
    w16 = {k: t[0].astype(BF16) for k, t in big.items()}

    gathered = []
    for i in range(depth):
        kind, j = i % 3, i // 3
        names = [("sb_w_qkv", "sb_w_o"), ("pool_w",), ("conv_w_in", "conv_w_out")][kind]
        tensors = [(w16[nm], j) for nm in names] + [(w16[nm], i) for nm in ("ffn_w_gate", "ffn_w_up", "ffn_w_down")]
        if kind == 2:
            tensors.append((conv_w, j))
        out = _all_gather(f"gather_l{i}", tensors)
        gathered.append(dict(zip(list(names) + ["ffn_w_gate", "ffn_w_up", "ffn_w_down"] + ["conv_w"] * (kind == 2), out)))

    def std_cols(wb):
        return jnp.transpose(wb, (1, 0, 2)).reshape(wb.shape[1], -1)

    saved = []
    xc = xs
    for i in range(depth):
        kind, j = i % 3, i // 3
        gw = gathered[i]
        s = {"x_in": xc}
        if kind == 0:
            h = _rms_fwd(f"norm_mix_l{i}", xc, norm_mix_g[i:i + 1], BF16)
            qkv = _mm_cols(f"qkv_l{i}", h, gw["sb_w_qkv"], F32)
            o32, o16 = _attn_fwd(f"attn_fwd_l{i}", qkv, sb_g_q[j:j + 1], sb_g_k[j:j + 1], H)
            xc = _mm_res(f"attn_out_l{i}", o16, gw["sb_w_o"].reshape(D, D), xc)
            s.update(h=h, qkv=qkv, o32=o32, o16=o16)
        elif kind == 1:
            hf = _rms_fwd(f"norm_mix_l{i}", xc, norm_mix_g[i:i + 1], F32)
            wp = jnp.transpose(gw["pool_w"].reshape(N_DEV, G, pool_rows, D // G), (1, 0, 2, 3)).reshape(G, D // G, D // G)
            xc, p = _pool_fwd(f"pool_fwd_l{i}", hf, xc, wp, pool_scale[j:j + 1])
            s.update(p=p, wp=wp)
        else:
            h = _rms_fwd(f"norm_mix_l{i}", xc, norm_mix_g[i:i + 1], BF16)
            bcx = _mm_cols(f"conv_in_l{i}", h, gw["conv_w_in"], F32)
            cw_full = jnp.transpose(gw["conv_w"], (1, 0, 2)).reshape(3, D)
            by = _conv_fwd(f"conv_fwd_l{i}", bcx, cw_full)
            xc = _mm_res(f"conv_out_l{i}", by, gw["conv_w_out"].reshape(D, D), xc)
            s.update(h=h, bcx=bcx, by=by, cw_full=cw_full)
        s["x_mid"] = xc
        h2 = _rms_fwd(f"norm_ffn_l{i}", xc, norm_ffn_g[i:i + 1], BF16)
        gate, up, act = _ffn_up(f"ffn_up_l{i}", h2, gw["ffn_w_gate"], gw["ffn_w_up"])
        xc = _ffn_down(f"ffn_down_l{i}", act, gw["ffn_w_down"], xc)
        s.update(h2=h2, gate=gate, up=up, act=act)
        saved.append(s)

    dx, dx16, loss_part = _loss_head("loss_head", xc, loss_target[0])

    outs = {k: None for k in big}

    def reduce_layer(i, grads):
        names = list(grads)
        got, kept = _swap_halves(f"swap_core_l{i}", [grads[nm][0] for nm in names])
        partial = [_add_to_bf16(f"chip_sum_{nm}_l{i}", a, b) for nm, a, b in zip(names, kept, got)]
        parts = _swap_chips(f"swap_chip_l{i}", partial)
        for nm, pt in zip(names, parts):
            w, m, v = big[nm]
            outs[nm] = _reduce_update(f"update_{nm}_l{i}", pt, w, m, v, grads[nm][1], outs[nm])

    d_mix, d_ffn = [None] * depth, [None] * depth
    small = {}
    for i in reversed(range(depth)):
        kind, j = i % 3, i // 3
        gw, s = gathered[i], saved[i]
        grads = {}
        dgate, dup = _ffn_dact(f"ffn_dact_l{i}", dx16, gw["ffn_w_down"], s["gate"], s["up"])
        grads["ffn_w_down"] = (_wgrad_rows(f"ffn_dwd_l{i}", s["act"], dx16), i)
        dh2 = _ffn_dh(f"ffn_dh_l{i}", dgate, dup, gw["ffn_w_gate"], gw["ffn_w_up"])
        dwg, dwu = _wgrad_cols(f"ffn_dwgu_l{i}", s["h2"], [dgate, dup])
        grads["ffn_w_gate"], grads["ffn_w_up"] = (dwg, i), (dwu, i)
        dx, dx16, d_ffn[i] = _rms_bwd(f"norm_ffn_bwd_l{i}", dh2, s["x_mid"], norm_ffn_g[i:i + 1], dx)
        if kind == 0:
            wo = gw["sb_w_o"]
            do16 = _mm_nt_rows(f"attn_do_l{i}", dx16, wo, BF16)
            grads["sb_w_o"] = (_wgrad_std(f"attn_dwo_l{i}", s["o16"], dx16).reshape(wo.shape), j)
            d3, dgq, dgk = _attn_bwd(f"attn_bwd_l{i}", s["qkv"], sb_g_q[j:j + 1], sb_g_k[j:j + 1], do16, s["o32"], H)
            small[("sb_g_q", j)], small[("sb_g_k", j)] = dgq, dgk
            dh = _mm_nt_sections(f"qkv_dh_l{i}", d3, std_cols(gw["sb_w_qkv"]))
            grads["sb_w_qkv"] = (_wgrad_sections(f"qkv_dw_l{i}", s["h"], d3, gw["sb_w_qkv"].shape[2]), j)
        elif kind == 1:
            dh, dwp, dps = _pool_bwd(f"pool_bwd_l{i}", dx, s["p"], s["wp"], pool_scale[j:j + 1])
            small[("pool_scale", j)] = dps
            dwp = jnp.transpose(dwp.reshape(G, N_DEV, pool_rows, D // G), (1, 0, 2, 3)).reshape(N_DEV, G * pool_rows, D // G)
            grads["pool_w"] = (dwp, j)
        else:
            wout = gw["conv_w_out"]
            dby = _mm_nt_rows(f"conv_dby_l{i}", dx16, wout, F32)
            grads["conv_w_out"] = (_wgrad_std(f"conv_dwout_l{i}", s["by"], dx16).reshape(wout.shape), j)
            d3, dcw = _conv_bwd(f"conv_bwd_l{i}", dby, s["bcx"], s["cw_full"])
            small[("conv_w", j)] = dcw[0:3]
            dh = _mm_nt_sections(f"conv_dh_l{i}", d3, std_cols(gw["conv_w_in"]))
            grads["conv_w_in"] = (_wgrad_sections(f"conv_dwin_l{i}", s["h"], d3, gw["conv_w_in"].shape[2]), j)
        dx, dx16, d_mix[i] = _rms_bwd(f"norm_mix_bwd_l{i}", dh, s["x_in"], norm_mix_g[i:i + 1], dx)
        reduce_layer(i, grads)

    n_sb, n_pool, n_conv = sb_g_q.shape[0], pool_scale.shape[0], conv_w.shape[0]
    pieces = [loss_part, jnp.concatenate(d_mix), jnp.concatenate(d_ffn),
              jnp.concatenate([small[("sb_g_q", j)] for j in range(n_sb)]),
              jnp.concatenate([small[("sb_g_k", j)] for j in range(n_sb)]),
              jnp.concatenate([small[("pool_scale", j)] for j in range(n_pool)]),
              jnp.stack([small[("conv_w", j)] for j in range(n_conv)])]
    n_small = sum(p.size for p in pieces)
    rows = -(-n_small // (8 * LANES)) * 8
    summed, loss = _all_reduce_small("reduce_small", _pack(pieces, rows))
    shapes = [norm_mix_g.shape, norm_ffn_g.shape, sb_g_q.shape, sb_g_k.shape, pool_scale.shape, (n_conv, 3, D)]
    g_mix, g_ffn, g_q, g_k, g_ps, g_cw_full = _unpack(summed, shapes, skip=8 * LANES)
    cshard = conv_w.shape[2]
    me = 4 * lax.axis_index("x") + 2 * lax.axis_index("y") + lax.axis_index("c")
    g_cw = lax.dynamic_slice_in_dim(g_cw_full, me * cshard, cshard, axis=2)
    small_names = ["norm_mix_g", "norm_ffn_g", "sb_g_q", "sb_g_k", "pool_scale", "conv_w"]
    small_w = [norm_mix_g, norm_ffn_g, sb_g_q, sb_g_k, pool_scale, conv_w]
    small_m = [m_norm_mix_g, m_norm_ffn_g, m_sb_g_q, m_sb_g_k, m_pool_scale, m_conv_w]
    small_v = [v_norm_mix_g, v_norm_ffn_g, v_sb_g_q, v_sb_g_k, v_pool_scale, v_conv_w]
    small_g = [g_mix, g_ffn, g_q, g_k, g_ps, g_cw]
    n_upd = sum(a.size for a in small_w)
    urows = -(-n_upd // (8 * LANES)) * 8
    sd, sm, sv = _adamw_small("update_small", _pack(small_w, urows), _pack(small_g, urows),
                              _pack(small_m, urows), _pack(small_v, urows))
    sshapes = [a.shape for a in small_w]
    res = {nm: (g, d, m_, v_) for nm, g, d, m_, v_ in
           zip(small_names, small_g, _unpack(sd, sshapes), _unpack(sm, sshapes), _unpack(sv, sshapes))}
    for nm, stacks in outs.items():
        shape = pool_w.shape if nm == "pool_w" else big[nm][0].shape
        res[nm] = tuple(a.reshape(shape) for a in stacks)

    order = ["norm_mix_g", "norm_ffn_g", "sb_w_qkv", "sb_g_q", "sb_g_k", "sb_w_o", "pool_w", "pool_scale",
             "conv_w_in", "conv_w", "conv_w_out", "ffn_w_gate", "ffn_w_up", "ffn_w_down"]
    return (loss.reshape(()), dx[None], *[res[nm][0] for nm in order], *[res[nm][1] for nm in order],
            *[res[nm][2] for nm in order], *[res[nm][3] for nm in order])
```

```python
import functools

import jax
import jax.numpy as jnp
from jax import lax
from jax.experimental import pallas as pl
from jax.experimental.pallas import tpu as pltpu

F32 = jnp.float32
BF16 = jnp.bfloat16

N_DEV = 8
N_XY = 4
HEAD_DIM = 128
LANES = 128
POOL_WINDOWS = (2, 4, 8, 16)
POOL_HALO = 16
CONV_HALO = 8
ATTN_BLOCK = 512
EPS = 1e-6
ADAM_LR = 0.001
ADAM_B1 = 0.9
ADAM_B2 = 0.999
ADAM_EPS = 1e-08
ADAM_WD = 0.01
ADAM_STEP = 10

VMEM_CAP_V7X = 56 * 1024 * 1024
VMEM_FLOOR = 32 * 1024 * 1024

NN = (((1,), (0,)), ((), ()))
NT = (((1,), (1,)), ((), ()))
TN = (((0,), (0,)), ((), ()))
MESH = pl.DeviceIdType.MESH
ANY = pl.BlockSpec(memory_space=pl.ANY)


def _dot(a, b, dims):
    return lax.dot_general(a, b, dims, preferred_element_type=F32)


def _params(step_bytes, scratch_bytes=0):
    need = 2 * step_bytes + scratch_bytes + 3 * step_bytes // 2 + (4 << 20)
    return pltpu.CompilerParams(vmem_limit_bytes=int(min(VMEM_CAP_V7X, max(VMEM_FLOOR, need))))


def _nbytes(shape, dtype):
    n = 1
    for s in shape:
        if s is not None:
            n *= s
    return n * jnp.dtype(dtype).itemsize


def _tile(n, target, mult=8):
    t = min(n, target)
    t -= t % mult
    while t > mult and n % t:
        t -= mult
    assert t >= mult and n % t == 0, (n, target, mult)
    return t


def _row_tile(rows, cols, itemsize, budget, mult=16):
    return _tile(rows, max(mult, budget // (cols * itemsize)), mult)


def _rms_fwd(name, x, g_row, out_dtype):
    T, D = x.shape
    tm = _tile(T, 512)

    def body(x_ref, g_ref, o_ref):
        xv = x_ref[...]
        r = lax.rsqrt(jnp.mean(xv * xv, axis=-1, keepdims=True) + EPS)
        o_ref[...] = (xv * r * g_ref[...]).astype(o_ref.dtype)

    return pl.pallas_call(
        body, name=name, grid=(T // tm,),
        in_specs=[pl.BlockSpec((tm, D), lambda i: (i, 0)), pl.BlockSpec((1, D), lambda i: (0, 0))],
        out_specs=pl.BlockSpec((tm, D), lambda i: (i, 0)),
        out_shape=jax.ShapeDtypeStruct((T, D), out_dtype),
        compiler_params=_params(_nbytes((tm, D), F32) * 2),
    )(x, g_row)


def _rms_bwd(name, dh, x, g_row, dres):
    T, D = x.shape
    tm = _tile(T, 256)

    def body(dh_ref, x_ref, g_ref, dres_ref, dx_ref, dx16_ref, dg_ref):
        xv = x_ref[...]
        r = lax.rsqrt(jnp.mean(xv * xv, axis=-1, keepdims=True) + EPS)
        xh = xv * r
        dhv = dh_ref[...]
        dy = dhv * g_ref[...]
        m = jnp.mean(dy * xh, axis=-1, keepdims=True)
        dx = dres_ref[...] + r * (dy - xh * m)
        dx_ref[...] = dx
        dx16_ref[...] = dx.astype(BF16)

        @pl.when(pl.program_id(0) == 0)
        def _():
            dg_ref[...] = jnp.zeros_like(dg_ref)

        dg_ref[...] += jnp.sum(dhv * xh, axis=0, keepdims=True)

    blk = pl.BlockSpec((tm, D), lambda i: (i, 0))
    row = pl.BlockSpec((1, D), lambda i: (0, 0))
    return pl.pallas_call(
        body, name=name, grid=(T // tm,),
        in_specs=[blk, blk, row, blk], out_specs=[blk, blk, row],
        out_shape=[jax.ShapeDtypeStruct((T, D), F32), jax.ShapeDtypeStruct((T, D), BF16),
                   jax.ShapeDtypeStruct((1, D), F32)],
        compiler_params=_params(_nbytes((tm, D), F32) * 5),
    )(dh, x, g_row, dres)


def _loss_head(name, y, target):
    T, D = y.shape
    tm = _tile(T, 256)

    def body(y_ref, t_ref, dy_ref, dy16_ref, acc_ref):
        e = y_ref[...] - t_ref[...]
        d = e * (1.0 / D)
        dy_ref[...] = d
        dy16_ref[...] = d.astype(BF16)
        s = (e * e).reshape(tm // 8, 8, D).sum(axis=0)
        part = s[:, 0:LANES]
        for k in range(1, D // LANES):
            part = part + s[:, k * LANES:(k + 1) * LANES]

        @pl.when(pl.program_id(0) == 0)
        def _():
            acc_ref[...] = jnp.zeros_like(acc_ref)

        acc_ref[...] += part * (0.5 / D)

    blk = pl.BlockSpec((tm, D), lambda i: (i, 0))
    return pl.pallas_call(
        body, name=name, grid=(T // tm,),
        in_specs=[blk, blk], out_specs=[blk, blk, pl.BlockSpec((8, LANES), lambda i: (0, 0))],
        out_shape=[jax.ShapeDtypeStruct((T, D), F32), jax.ShapeDtypeStruct((T, D), BF16),
                   jax.ShapeDtypeStruct((8, LANES), F32)],
        compiler_params=_params(_nbytes((tm, D), F32) * 4),
    )(y, target)


def _mm_cols(name, a, w, out_dtype):
    T, K = a.shape
    nb, _, ns = w.shape
    tm = _tile(T, 1024)

    def body(a_ref, w_ref, o_ref):
        o_ref[...] = _dot(a_ref[...], w_ref[...], NN).astype(o_ref.dtype)

    step = _nbytes((tm, K), BF16) + _nbytes((K, ns), BF16) + _nbytes((tm, ns), out_dtype)
    return pl.pallas_call(
        body, name=name, grid=(nb, T // tm),
        in_specs=[pl.BlockSpec((tm, K), lambda b, i: (i, 0)),
                  pl.BlockSpec((None, K, ns), lambda b, i: (b, 0, 0))],
        out_specs=pl.BlockSpec((None, tm, ns), lambda b, i: (b, i, 0)),
        out_shape=jax.ShapeDtypeStruct((nb, T, ns), out_dtype),
        compiler_params=_params(step),
    )(a, w)


def _ffn_up(name, h, wg, wu):
    T, K = h.shape
    nb, _, fs = wg.shape
    tm = _tile(T, 1024)

    def body(a_ref, wg_ref, wu_ref, g_ref, u_ref, act_ref):
        av = a_ref[...]
        g = _dot(av, wg_ref[...], NN)
        u = _dot(av, wu_ref[...], NN)
        g_ref[...] = g
        u_ref[...] = u
        act_ref[...] = (g * jax.nn.sigmoid(g) * u).astype(BF16)

    wspec = pl.BlockSpec((None, K, fs), lambda b, i: (b, 0, 0))
    ospec = pl.BlockSpec((None, tm, fs), lambda b, i: (b, i, 0))
    osh = lambda dt: jax.ShapeDtypeStruct((nb, T, fs), dt)
    step = _nbytes((tm, K), BF16) + 2 * _nbytes((K, fs), BF16) + _nbytes((tm, fs), BF16) + 2 * _nbytes((tm, fs), F32)
    return pl.pallas_call(
        body, name=name, grid=(nb, T // tm),
        in_specs=[pl.BlockSpec((tm, K), lambda b, i: (i, 0)), wspec, wspec],
        out_specs=[ospec, ospec, ospec], out_shape=[osh(F32), osh(F32), osh(BF16)],
        compiler_params=_params(step),
    )(h, wg, wu)


def _mm_res(name, a, w, res):
    T, K = a.shape
    N = w.shape[1]
    tm, tn = _tile(T, 512), _tile(N, 1024, LANES)

    def body(a_ref, w_ref, r_ref, o_ref):
        o_ref[...] = r_ref[...] + _dot(a_ref[...], w_ref[...], NN)

    step = _nbytes((tm, K), BF16) + _nbytes((K, tn), BF16) + 2 * _nbytes((tm, tn), F32)
    return pl.pallas_call(
        body, name=name, grid=(N // tn, T // tm),
        in_specs=[pl.BlockSpec((tm, K), lambda j, i: (i, 0)), pl.BlockSpec((K, tn), lambda j, i: (0, j)),
                  pl.BlockSpec((tm, tn), lambda j, i: (i, j))],
        out_specs=pl.BlockSpec((tm, tn), lambda j, i: (i, j)),
        out_shape=jax.ShapeDtypeStruct((T, N), F32),
        compiler_params=_params(step),
    )(a, w, res)


def _ffn_down(name, act, wd, res):
    nb, T, fs = act.shape
    N = wd.shape[2]
    tm, tn = _tile(T, 1024), _tile(N, 1024, LANES)

    def body(a_ref, w_ref, r_ref, o_ref, acc_ref):
        b = pl.program_id(2)

        @pl.when(b == 0)
        def _():
            acc_ref[...] = r_ref[...]

        acc_ref[...] += _dot(a_ref[...], w_ref[...], NN)

        @pl.when(b == nb - 1)
        def _():
            o_ref[...] = acc_ref[...]

    step = _nbytes((tm, fs), BF16) + _nbytes((fs, tn), BF16) + 2 * _nbytes((tm, tn), F32)
    return pl.pallas_call(
        body, name=name, grid=(T // tm, N // tn, nb),
        in_specs=[pl.BlockSpec((None, tm, fs), lambda i, j, b: (b, i, 0)),
                  pl.BlockSpec((None, fs, tn), lambda i, j, b: (b, 0, j)),
                  pl.BlockSpec((tm, tn), lambda i, j, b: (i, j))],
        out_specs=pl.BlockSpec((tm, tn), lambda i, j, b: (i, j)),
        out_shape=jax.ShapeDtypeStruct((T, N), F32),
        scratch_shapes=[pltpu.VMEM((tm, tn), F32)],
        compiler_params=_params(step, _nbytes((tm, tn), F32)),
    )(act, wd, res)


def _mm_nt_rows(name, g, w, out_dtype):
    T, N = g.shape
    nb, ks, _ = w.shape
    tm = _tile(T, 1024)

    def body(g_ref, w_ref, o_ref):
        o_ref[...] = _dot(g_ref[...], w_ref[...], NT).astype(o_ref.dtype)

    step = _nbytes((tm, N), BF16) + _nbytes((ks, N), BF16) + _nbytes((tm, ks), out_dtype)
    return pl.pallas_call(
        body, name=name, grid=(nb, T // tm),
        in_specs=[pl.BlockSpec((tm, N), lambda b, i: (i, 0)),
                  pl.BlockSpec((None, ks, N), lambda b, i: (b, 0, 0))],
        out_specs=pl.BlockSpec((tm, ks), lambda b, i: (i, b)),
        out_shape=jax.ShapeDtypeStruct((T, nb * ks), out_dtype),
        compiler_params=_params(step),
    )(g, w)


def _ffn_dact(name, dx16, wd, gate, up):
    T, N = dx16.shape
    nb, fs, _ = wd.shape
    tm = _tile(T, 1024)

    def body(g_ref, w_ref, gate_ref, up_ref, dg_ref, du_ref):
        dact = _dot(g_ref[...], w_ref[...], NT)
        gt = gate_ref[...]
        s = jax.nn.sigmoid(gt)
        silu = gt * s
        dg_ref[...] = (dact * up_ref[...] * (s * (1.0 + gt * (1.0 - s)))).astype(BF16)
        du_ref[...] = (dact * silu).astype(BF16)

    aspec = pl.BlockSpec((None, tm, fs), lambda b, i: (b, i, 0))
    osh = jax.ShapeDtypeStruct((nb, T, fs), BF16)
    step = _nbytes((tm, N), BF16) + _nbytes((fs, N), BF16) + 2 * _nbytes((tm, fs), BF16) + 3 * _nbytes((tm, fs), F32)
    return pl.pallas_call(
        body, name=name, grid=(nb, T // tm),
        in_specs=[pl.BlockSpec((tm, N), lambda b, i: (i, 0)),
                  pl.BlockSpec((None, fs, N), lambda b, i: (b, 0, 0)), aspec, aspec],
        out_specs=[aspec, aspec], out_shape=[osh, osh],
        compiler_params=_params(step),
    )(dx16, wd, gate, up)


def _ffn_dh(name, dgate, dup, wg, wu):
    nb, T, fs = dgate.shape
    D = wg.shape[1]
    tm, tn = _tile(T, 1024), _tile(D, 1024, LANES)

    def body(dg_ref, du_ref, wg_ref, wu_ref, o_ref, acc_ref):
        b = pl.program_id(2)

        @pl.when(b == 0)
        def _():
            acc_ref[...] = jnp.zeros_like(acc_ref)

        acc_ref[...] += _dot(dg_ref[...], wg_ref[...], NT) + _dot(du_ref[...], wu_ref[...], NT)

        @pl.when(b == nb - 1)
        def _():
            o_ref[...] = acc_ref[...]

    aspec = pl.BlockSpec((None, tm, fs), lambda i, j, b: (b, i, 0))
    wspec = pl.BlockSpec((None, tn, fs), lambda i, j, b: (b, j, 0))
    step = 2 * _nbytes((tm, fs), BF16) + 2 * _nbytes((tn, fs), BF16) + _nbytes((tm, tn), F32)
    return pl.pallas_call(
        body, name=name, grid=(T // tm, D // tn, nb),
        in_specs=[aspec, aspec, wspec, wspec],
        out_specs=pl.BlockSpec((tm, tn), lambda i, j, b: (i, j)),
        out_shape=jax.ShapeDtypeStruct((T, D), F32),
        scratch_shapes=[pltpu.VMEM((tm, tn), F32)],
        compiler_params=_params(step, _nbytes((tm, tn), F32)),
    )(dgate, dup, wg, wu)


def _mm_nt_sections(name, g3, w_std):
    ns_, T, Ds = g3.shape
    D = w_std.shape[0]
    tm, tn, tk = _tile(T, 1024), _tile(D, 1024, LANES), _tile(Ds, 1024, LANES)
    nkk = Ds // tk
    nk = ns_ * nkk

    def body(g_ref, w_ref, o_ref, acc_ref):
        r = pl.program_id(2)

        @pl.when(r == 0)
        def _():
            acc_ref[...] = jnp.zeros_like(acc_ref)

        acc_ref[...] += _dot(g_ref[...], w_ref[...], NT)

        @pl.when(r == nk - 1)
        def _():
            o_ref[...] = acc_ref[...]

    step = _nbytes((tm, tk), BF16) + _nbytes((tn, tk), BF16) + _nbytes((tm, tn), F32)
    return pl.pallas_call(
        body, name=name, grid=(T // tm, D // tn, nk),
        in_specs=[pl.BlockSpec((None, tm, tk), lambda i, j, r: (r // nkk, i, r % nkk)),
                  pl.BlockSpec((tn, tk), lambda i, j, r: (j, r))],
        out_specs=pl.BlockSpec((tm, tn), lambda i, j, r: (i, j)),
        out_shape=jax.ShapeDtypeStruct((T, D), F32),
        scratch_shapes=[pltpu.VMEM((tm, tn), F32)],
        compiler_params=_params(step, _nbytes((tm, tn), F32)),
    )(g3, w_std)


def _wgrad(name, grid, a, a_spec, gs, g_spec, out_shape, o_spec, acc_shape):
    n = len(gs)
    nt = grid[-1]

    def body(*refs):
        a_ref, g_refs, o_refs, acc_refs = refs[0], refs[1:1 + n], refs[1 + n:1 + 2 * n], refs[1 + 2 * n:]
        t = pl.program_id(len(grid) - 1)
        av = a_ref[...]
        for g_ref, o_ref, acc_ref in zip(g_refs, o_refs, acc_refs):
            @pl.when(t == 0)
            def _():
                acc_ref[...] = jnp.zeros_like(acc_ref)

            acc_ref[...] += _dot(av, g_ref[...], TN)

            @pl.when(t == nt - 1)
            def _():
                o_ref[...] = acc_ref[...]

    step = _nbytes(a_spec.block_shape, BF16) + n * (_nbytes(g_spec.block_shape, BF16) + _nbytes(acc_shape, F32))
    outs = pl.pallas_call(
        body, name=name, grid=grid,
        in_specs=[a_spec] + [g_spec] * n, out_specs=[o_spec] * n,
        out_shape=[jax.ShapeDtypeStruct(out_shape, F32)] * n,
        scratch_shapes=[pltpu.VMEM(acc_shape, F32)] * n,
        compiler_params=_params(step, n * _nbytes(acc_shape, F32)),
    )(a, *gs)
    return outs


def _wgrad_cols(name, a, gs):
    T, K = a.shape
    nb, _, ns = gs[0].shape
    tk, tt = _tile(K, 1024, LANES), _tile(T, 1024)
    return _wgrad(name, (nb, K // tk, T // tt), a,
                  pl.BlockSpec((tt, tk), lambda b, k, t: (t, k)), gs,
                  pl.BlockSpec((None, tt, ns), lambda b, k, t: (b, t, 0)),
                  (nb, K, ns), pl.BlockSpec((None, tk, ns), lambda b, k, t: (b, k, 0)), (tk, ns))


def _wgrad_rows(name, act, dx16):
    nb, T, fs = act.shape
    N = dx16.shape[1]
    tn, tt = _tile(N, 1024, LANES), _tile(T, 1024)
    return _wgrad(name, (nb, N // tn, T // tt), act,
                  pl.BlockSpec((None, tt, fs), lambda b, j, t: (b, t, 0)), [dx16],
                  pl.BlockSpec((tt, tn), lambda b, j, t: (t, j)),
                  (nb, fs, N), pl.BlockSpec((None, fs, tn), lambda b, j, t: (b, 0, j)), (fs, tn))[0]


def _wgrad_std(name, a, g):
    T, K = a.shape
    N = g.shape[1]
    tk, tn, tt = _tile(K, 1024, LANES), _tile(N, 1024, LANES), _tile(T, 1024)
    return _wgrad(name, (K // tk, N // tn, T // tt), a,
                  pl.BlockSpec((tt, tk), lambda k, j, t: (t, k)), [g],
                  pl.BlockSpec((tt, tn), lambda k, j, t: (t, j)),
                  (K, N), pl.BlockSpec((tk, tn), lambda k, j, t: (k, j)), (tk, tn))[0]


def _wgrad_sections(name, a, g3, ns):
    T, K = a.shape
    nsec, _, Ds = g3.shape
    cw = 256 if (ns % 256 == 0 and Ds % 256 == 0) else LANES
    per_sec, per_shard = Ds // cw, ns // cw
    nb = nsec * Ds // ns
    tk, tt = _tile(K, 1024, LANES), _tile(T, 1024)
    return _wgrad(name, (nsec * per_sec, K // tk, T // tt), a,
                  pl.BlockSpec((tt, tk), lambda p, k, t: (t, k)), [g3],
                  pl.BlockSpec((None, tt, cw), lambda p, k, t: (p // per_sec, t, p % per_sec)),
                  (nb, K, ns), pl.BlockSpec((None, tk, cw), lambda p, k, t: (p // per_shard, k, p % per_shard)),
                  (tk, cw))[0]


def _tri2():
    r = lax.broadcasted_iota(jnp.int32, (LANES, 2 * LANES), 0)
    c = lax.broadcasted_iota(jnp.int32, (LANES, 2 * LANES), 1)
    return jnp.where((r >= c) | (c >= LANES), 1.0, 0.0).astype(BF16)


def _suffix_sums(v, tri):
    hi = v.astype(BF16)
    lo = (v - hi.astype(F32)).astype(BF16)
    both = _dot(hi, tri, NN) + _dot(lo, tri, NN)
    return both[:, :LANES], both[:, LANES:]


def _suffix_sums_wide(v, carry, tri):
    pieces = [None] * (v.shape[1] // LANES)
    for u in reversed(range(len(pieces))):
        incl, tot = _suffix_sums(v[:, u * LANES:(u + 1) * LANES], tri)
        pieces[u] = incl + carry
        carry = carry + tot
    return jnp.concatenate(pieces, axis=1), carry


def _sb_group(z, mask, carry, tri):
    lsn = -(jnp.maximum(z, 0.0) + jnp.log(1.0 + jnp.exp(-jnp.abs(z))))
    lk = lsn if mask is None else jnp.where(mask, lsn, 0.0)
    incl, carry = _suffix_sums_wide(lk, carry, tri)
    a = jnp.exp(z + lsn + (incl - lk))
    return (a if mask is None else jnp.where(mask, a, 0.0)), lsn, carry


def _head_norm_store(src_ref, g_ref, dst_ref, rows, chunk):
    def step(i, _):
        r0 = pl.multiple_of(i * chunk, chunk)
        v = src_ref[pl.ds(r0, chunk), :]
        r = lax.rsqrt(jnp.mean(v * v, axis=-1, keepdims=True) + EPS)
        dst_ref[pl.ds(r0, chunk), :] = (v * r * g_ref[...]).astype(dst_ref.dtype)
        return 0

    lax.fori_loop(0, rows // chunk, step, 0)


def _qkv_specs(T, ns, H):
    cps = ns // HEAD_DIM

    def spec(sec):
        return pl.BlockSpec((None, T, HEAD_DIM), lambda h: ((sec * H + h) // cps, 0, (sec * H + h) % cps))

    return [spec(0), spec(1), spec(2)]


def _attn_fwd(name, qkv, gq, gk, H):
    _, T, ns = qkv.shape
    D = H * HEAD_DIM
    BQ = _tile(T, ATTN_BLOCK, LANES)
    scale = HEAD_DIM ** -0.5

    def body(q_ref, k_ref, v_ref, gq_ref, gk_ref, o32_ref, o16_ref, qn, kn, vb):
        _head_norm_store(q_ref, gq_ref, qn, T, BQ)
        _head_norm_store(k_ref, gk_ref, kn, T, BQ)
        vb[...] = v_ref[...].astype(BF16)
        tri = _tri2()
        causal = lax.broadcasted_iota(jnp.int32, (BQ, BQ), 1) < lax.broadcasted_iota(jnp.int32, (BQ, BQ), 0)

        def qloop(qi, _):
            t0 = pl.multiple_of(qi * BQ, BQ)
            qb = qn[pl.ds(t0, BQ), :]

            def step(grp, carry, mask):
                o_acc, o_low, cr = carry
                s0 = pl.multiple_of(grp * BQ, BQ)
                z = _dot(qb, kn[pl.ds(s0, BQ), :], NT) * scale
                a, _, cr = _sb_group(z, mask, cr, tri)
                vj = vb[pl.ds(s0, BQ), :]
                a_hi = a.astype(BF16)
                o_acc = o_acc + _dot(a_hi, vj, NN)
                o_low = o_low + _dot((a - a_hi.astype(F32)).astype(BF16), vj, NN)
                return o_acc, o_low, cr

            zero = jnp.zeros((BQ, LANES), F32)
            first = step(qi, (zero, zero, zero), causal)
            o_acc, o_low, _ = lax.fori_loop(0, qi, lambda jj, c: step(qi - 1 - jj, c, None), first)
            o32_ref[pl.ds(t0, BQ), :] = o_acc + o_low
            o16_ref[pl.ds(t0, BQ), :] = o_acc.astype(BF16)
            return 0

        lax.fori_loop(0, T // BQ, qloop, 0)

    gspec = pl.BlockSpec((1, HEAD_DIM), lambda h: (0, 0))
    ospec = pl.BlockSpec((T, HEAD_DIM), lambda h: (0, h))
    step = 3 * _nbytes((T, HEAD_DIM), F32) + _nbytes((T, HEAD_DIM), F32) + _nbytes((T, HEAD_DIM), BF16)
    return pl.pallas_call(
        body, name=name, grid=(H,),
        in_specs=_qkv_specs(T, ns, H) + [gspec, gspec],
        out_specs=[ospec, ospec],
        out_shape=[jax.ShapeDtypeStruct((T, D), F32), jax.ShapeDtypeStruct((T, D), BF16)],
        scratch_shapes=[pltpu.VMEM((T, HEAD_DIM), BF16)] * 3,
        compiler_params=_params(step, 3 * _nbytes((T, HEAD_DIM), BF16) + 10 * _nbytes((BQ, BQ), F32)),
    )(qkv, qkv, qkv, gq, gk)


def _attn_bwd(name, qkv, gq, gk, do16, o32, H):
    _, T, ns = qkv.shape
    D = H * HEAD_DIM
    BQ = _tile(T, ATTN_BLOCK, LANES)
    scale = HEAD_DIM ** -0.5

    def body(q_ref, k_ref, v_ref, gq_ref, gk_ref, do_ref, o_ref, d3_ref, dgq_ref, dgk_ref,
             qn, kn, vb, dqn, dkn, dv):
        h = pl.program_id(0)
        _head_norm_store(q_ref, gq_ref, qn, T, BQ)
        _head_norm_store(k_ref, gk_ref, kn, T, BQ)
        vb[...] = v_ref[...].astype(BF16)
        dkn[...] = jnp.zeros_like(dkn)
        dv[...] = jnp.zeros_like(dv)
        tri = _tri2()
        causal = lax.broadcasted_iota(jnp.int32, (BQ, BQ), 1) < lax.broadcasted_iota(jnp.int32, (BQ, BQ), 0)

        def qloop(qi, _):
            t0 = pl.multiple_of(qi * BQ, BQ)
            qb = qn[pl.ds(t0, BQ), :]
            dob = do_ref[pl.ds(t0, BQ), :]
            total = jnp.sum(dob.astype(F32) * o_ref[pl.ds(t0, BQ), :], axis=-1, keepdims=True)

            def step(grp, carry, mask):
                dq_acc, cr, crd = carry
                s0 = pl.multiple_of(grp * BQ, BQ)
                kb = kn[pl.ds(s0, BQ), :]
                vj = vb[pl.ds(s0, BQ), :]
                z = _dot(qb, kb, NT) * scale
                a, lsn, cr = _sb_group(z, mask, cr, tri)
                dla = a * _dot(dob, vj, NT)
                later, crd = _suffix_sums_wide(dla, crd, tri)
                sig = jnp.exp(z + lsn)
                d_keep = (total - later) * sig
                dz = (dla * (1.0 - sig) - (d_keep if mask is None else jnp.where(mask, d_keep, 0.0))) * scale
                dzb = dz.astype(BF16)
                dq_acc = dq_acc + _dot(dzb, kb, NN)
                dkn[pl.ds(s0, BQ), :] += _dot(dzb, qb, TN)
                dv[pl.ds(s0, BQ), :] += _dot(a.astype(BF16), dob, TN)
                return dq_acc, cr, crd

            zero = jnp.zeros((BQ, LANES), F32)
            first = step(qi, (zero, zero, zero), causal)
            dq_acc, _, _ = lax.fori_loop(0, qi, lambda jj, c: step(qi - 1 - jj, c, None), first)
            dqn[pl.ds(t0, BQ), :] = dq_acc
            return 0

        lax.fori_loop(0, T // BQ, qloop, 0)

        @pl.when(h == 0)
        def _():
            dgq_ref[...] = jnp.zeros_like(dgq_ref)
            dgk_ref[...] = jnp.zeros_like(dgk_ref)

        def norm_bwd(src_ref, g_ref, dy_ref, sec, dg_ref):
            def step(i, _):
                r0 = pl.multiple_of(i * BQ, BQ)
                v = src_ref[pl.ds(r0, BQ), :]
                r = lax.rsqrt(jnp.mean(v * v, axis=-1, keepdims=True) + EPS)
                vh = v * r
                dyo = dy_ref[pl.ds(r0, BQ), :]
                dy = dyo * g_ref[...]
                m = jnp.mean(dy * vh, axis=-1, keepdims=True)
                d3_ref[sec, pl.ds(r0, BQ), :] = (r * (dy - vh * m)).astype(BF16)
                dg_ref[...] += jnp.sum(dyo * vh, axis=0, keepdims=True)
                return 0

            lax.fori_loop(0, T // BQ, step, 0)

        norm_bwd(q_ref, gq_ref, dqn, 0, dgq_ref)
        norm_bwd(k_ref, gk_ref, dkn, 1, dgk_ref)
        d3_ref[2] = dv[...].astype(BF16)

    gspec = pl.BlockSpec((1, HEAD_DIM), lambda h: (0, 0))
    hspec = pl.BlockSpec((T, HEAD_DIM), lambda h: (0, h))
    step = (3 * _nbytes((T, HEAD_DIM), F32) + _nbytes((T, HEAD_DIM), BF16) + _nbytes((T, HEAD_DIM), F32)
            + 3 * _nbytes((T, HEAD_DIM), BF16))
    scratch = 3 * _nbytes((T, HEAD_DIM), BF16) + 3 * _nbytes((T, HEAD_DIM), F32)
    return pl.pallas_call(
        body, name=name, grid=(H,),
        in_specs=_qkv_specs(T, ns, H) + [gspec, gspec, hspec, hspec],
        out_specs=[pl.BlockSpec((3, T, HEAD_DIM), lambda h: (0, 0, h)), gspec, gspec],
        out_shape=[jax.ShapeDtypeStruct((3, T, D), BF16), jax.ShapeDtypeStruct((1, HEAD_DIM), F32),
                   jax.ShapeDtypeStruct((1, HEAD_DIM), F32)],
        scratch_shapes=[pltpu.VMEM((T, HEAD_DIM), BF16)] * 3 + [pltpu.VMEM((T, HEAD_DIM), F32)] * 3,
        compiler_params=_params(step, scratch + 12 * _nbytes((BQ, BQ), F32)),
    )(qkv, qkv, qkv, gq, gk, do16, o32)


def _by_group(g, vals):
    out = vals[-1]
    for k in range(len(vals) - 2, -1, -1):
        out = jnp.where(g == k, vals[k], out)
    return out


def _pool_fwd(name, hf, x, wp, scale_row):
    T, D = x.shape
    G = len(POOL_WINDOWS)
    C = D // G
    tm = _tile(T, 512, POOL_HALO)
    hb = tm // POOL_HALO

    def body(h_ref, halo_ref, x_ref, w_ref, s_ref, xo_ref, p_ref):
        g, i = pl.program_id(0), pl.program_id(1)
        hv = h_ref[...]
        ext = jnp.concatenate([halo_ref[...] * (i > 0).astype(F32), hv], axis=0)
        sums, acc = [], ext
        for k in range(len(POOL_WINDOWS)):
            acc = acc + pltpu.roll(acc, 1 << k, 0)
            sums.append(acc)
        ws = _by_group(g, sums)[POOL_HALO:]
        t = i * tm + lax.broadcasted_iota(jnp.int32, (tm, 1), 0)
        cnt = jnp.minimum(t + 1, lax.shift_left(jnp.int32(2), g)).astype(F32)
        p = (ws / cnt - hv).astype(BF16)
        p_ref[...] = p
        xo_ref[...] = x_ref[...] + _dot(p, w_ref[...], NN) * s_ref[...]

    blk = pl.BlockSpec((tm, C), lambda g, i: (i, g))
    step = 3 * _nbytes((tm, C), F32) + _nbytes((tm, C), BF16) + _nbytes((C, C), BF16)
    return pl.pallas_call(
        body, name=name, grid=(G, T // tm),
        in_specs=[blk, pl.BlockSpec((POOL_HALO, C), lambda g, i: (jnp.maximum(i * hb - 1, 0), g)), blk,
                  pl.BlockSpec((None, C, C), lambda g, i: (g, 0, 0)), pl.BlockSpec((1, C), lambda g, i: (0, g))],
        out_specs=[blk, blk],
        out_shape=[jax.ShapeDtypeStruct((T, D), F32), jax.ShapeDtypeStruct((T, D), BF16)],
        compiler_params=_params(step + 6 * _nbytes((tm, C), F32)),
    )(hf, hf, x, wp, scale_row)


def _pool_bwd(name, dx, p, wp, scale_row):
    T, D = dx.shape
    G = len(POOL_WINDOWS)
    C = D // G
    tm = _tile(T, 512, POOL_HALO)
    hb = tm // POOL_HALO
    nt = T // tm
    n = tm + POOL_HALO

    def body(dx_ref, halo_ref, p_ref, w_ref, s_ref, dh_ref, dw_ref, ds_ref):
        g, i = pl.program_id(0), pl.program_id(1)
        dxv = dx_ref[...]
        dxe = jnp.concatenate([dxv, halo_ref[...] * (i < nt - 1).astype(F32)], axis=0)
        dyp = (dxe * s_ref[...]).astype(BF16)
        wv = w_ref[...]
        dp = _dot(dyp, wv, NT)
        t = i * tm + lax.broadcasted_iota(jnp.int32, (n, 1), 0)
        cnt = jnp.minimum(t + 1, lax.shift_left(jnp.int32(2), g)).astype(F32)
        sums, acc = [], dp / cnt
        for k in range(len(POOL_WINDOWS)):
            acc = acc + pltpu.roll(acc, n - (1 << k), 0)
            sums.append(acc)
        dh_ref[...] = _by_group(g, sums)[:tm] - dp[:tm]
        pv = p_ref[...]

        @pl.when(i == 0)
        def _():
            dw_ref[...] = jnp.zeros_like(dw_ref)
            ds_ref[...] = jnp.zeros_like(ds_ref)

        ds_ref[...] += jnp.sum(dxv * _dot(pv, wv, NN), axis=0, keepdims=True)
        dw_ref[...] += _dot(pv, dyp[:tm], TN)

    blk = pl.BlockSpec((tm, C), lambda g, i: (i, g))
    step = 2 * _nbytes((tm, C), F32) + _nbytes((tm, C), BF16) + _nbytes((C, C), BF16) + _nbytes((C, C), F32)
    return pl.pallas_call(
        body, name=name, grid=(G, nt),
        in_specs=[blk, pl.BlockSpec((POOL_HALO, C), lambda g, i: (jnp.minimum((i + 1) * hb, T // POOL_HALO - 1), g)),
                  blk, pl.BlockSpec((None, C, C), lambda g, i: (g, 0, 0)), pl.BlockSpec((1, C), lambda g, i: (0, g))],
        out_specs=[blk, pl.BlockSpec((None, C, C), lambda g, i: (g, 0, 0)), pl.BlockSpec((1, C), lambda g, i: (0, g))],
        out_shape=[jax.ShapeDtypeStruct((T, D), F32), jax.ShapeDtypeStruct((G, C, C), F32),
                   jax.ShapeDtypeStruct((1, D), F32)],
        compiler_params=_params(step + 8 * _nbytes((tm, C), F32)),
    )(dx, dx, p, wp, scale_row)


def _section_spec(rows, cw, ns, D, sec, row_map):
    per = ns // cw

    def imap(j, i):
        c = (sec * D) // cw + j
        return (c // per, row_map(i), c % per)

    return pl.BlockSpec((None, rows, cw), imap)


def _conv_fwd(name, bcx, cw_full):
    _, T, ns = bcx.shape
    D = cw_full.shape[1]
    cw = 256 if (ns % 256 == 0 and D % 256 == 0) else LANES
    tm = _tile(T, 512, CONV_HALO)
    hb = tm // CONV_HALO

    def body(b_ref, c_ref, u_ref, ch_ref, uh_ref, w_ref, o_ref):
        i = pl.program_id(1)
        gm = c_ref[...] * u_ref[...]
        ext = jnp.concatenate([ch_ref[...] * uh_ref[...] * (i > 0).astype(F32), gm], axis=0)
        w0, w1, w2 = w_ref[0:1, :], w_ref[1:2, :], w_ref[2:3, :]
        y = w2 * gm + w1 * pltpu.roll(ext, 1, 0)[CONV_HALO:] + w0 * pltpu.roll(ext, 2, 0)[CONV_HALO:]
        o_ref[...] = (b_ref[...] * y).astype(BF16)

    main = lambda i: i
    prev = lambda i: jnp.maximum(i * hb - 1, 0)
    return pl.pallas_call(
        body, name=name, grid=(D // cw, T // tm),
        in_specs=[_section_spec(tm, cw, ns, D, 0, main), _section_spec(tm, cw, ns, D, 1, main),
                  _section_spec(tm, cw, ns, D, 2, main), _section_spec(CONV_HALO, cw, ns, D, 1, prev),
                  _section_spec(CONV_HALO, cw, ns, D, 2, prev), pl.BlockSpec((3, cw), lambda j, i: (0, j))],
        out_specs=pl.BlockSpec((tm, cw), lambda j, i: (i, j)),
        out_shape=jax.ShapeDtypeStruct((T, D), BF16),
        compiler_params=_params(8 * _nbytes((tm, cw), F32)),
    )(bcx, bcx, bcx, bcx, bcx, cw_full)


def _conv_bwd(name, dby, bcx, cw_full):
    _, T, ns = bcx.shape
    D = cw_full.shape[1]
    cw = 256 if (ns % 256 == 0 and D % 256 == 0) else LANES
    tm = _tile(T, 512, CONV_HALO)
    hb = tm // CONV_HALO
    nt = T // tm
    n = tm + CONV_HALO

    def body(dby_ref, dbyh_ref, b_ref, bh_ref, c_ref, u_ref, ch_ref, uh_ref, w_ref, d3_ref, dw_ref):
        i = pl.program_id(1)
        w0, w1, w2 = w_ref[0:1, :], w_ref[1:2, :], w_ref[2:3, :]
        bv, cv, uv, dbyv = b_ref[...], c_ref[...], u_ref[...], dby_ref[...]
        gm = cv * uv
        ext_g = jnp.concatenate([ch_ref[...] * uh_ref[...] * (i > 0).astype(F32), gm], axis=0)
        g1 = pltpu.roll(ext_g, 1, 0)[CONV_HALO:]
        g2 = pltpu.roll(ext_g, 2, 0)[CONV_HALO:]
        y = w2 * gm + w1 * g1 + w0 * g2
        dy = dbyv * bv
        ext_dy = jnp.concatenate([dy, dbyh_ref[...] * bh_ref[...] * (i < nt - 1).astype(F32)], axis=0)
        dg = w2 * dy + w1 * pltpu.roll(ext_dy, n - 1, 0)[:tm] + w0 * pltpu.roll(ext_dy, n - 2, 0)[:tm]
        d3_ref[0] = (dbyv * y).astype(BF16)
        d3_ref[1] = (dg * uv).astype(BF16)
        d3_ref[2] = (dg * cv).astype(BF16)

        @pl.when(i == 0)
        def _():
            dw_ref[...] = jnp.zeros_like(dw_ref)

        rows = [jnp.sum(dy * v, axis=0, keepdims=True) for v in (g2, g1, gm)]
        dw_ref[...] += jnp.concatenate(rows + [jnp.zeros((8 - len(rows), cw), F32)], axis=0)

    main = lambda i: i
    prev = lambda i: jnp.maximum(i * hb - 1, 0)
    nxt = lambda i: jnp.minimum((i + 1) * hb, T // CONV_HALO - 1)
    return pl.pallas_call(
        body, name=name, grid=(D // cw, nt),
        in_specs=[pl.BlockSpec((tm, cw), lambda j, i: (i, j)), pl.BlockSpec((CONV_HALO, cw), lambda j, i: (nxt(i), j)),
                  _section_spec(tm, cw, ns, D, 0, main), _section_spec(CONV_HALO, cw, ns, D, 0, nxt),
                  _section_spec(tm, cw, ns, D, 1, main), _section_spec(tm, cw, ns, D, 2, main),
                  _section_spec(CONV_HALO, cw, ns, D, 1, prev), _section_spec(CONV_HALO, cw, ns, D, 2, prev),
                  pl.BlockSpec((3, cw), lambda j, i: (0, j))],
        out_specs=[pl.BlockSpec((3, tm, cw), lambda j, i: (0, i, j)), pl.BlockSpec((8, cw), lambda j, i: (0, j))],
        out_shape=[jax.ShapeDtypeStruct((3, T, D), BF16), jax.ShapeDtypeStruct((8, D), F32)],
        compiler_params=_params(14 * _nbytes((tm, cw), F32)),
    )(dby, dby, bcx, bcx, bcx, bcx, bcx, bcx, cw_full)


def _position():
    return lax.axis_index("x"), lax.axis_index("y"), lax.axis_index("c")


def _all_gather(name, tensors):
    n = len(tensors)
    shapes = [a.shape[1:] if idx is not None else a.shape for a, idx in tensors]

    def body(*refs):
        srcs, outs = refs[:n], refs[n:2 * n]
        send_sems, recv_sems, local_sems = refs[2 * n:]
        x, y, c = _position()
        me, sibling = (x, y, c), (x, y, 1 - c)
        chips = [(1 - x, y), (x, 1 - y), (1 - x, 1 - y)]

        def slot(out, p):
            return out.at[4 * p[0] + 2 * p[1] + p[2]]

        def copy(t, k, block, to, src=None):
            return pltpu.make_async_remote_copy(
                src_ref=slot(outs[t], block) if src is None else src, dst_ref=slot(outs[t], block),
                send_sem=send_sems.at[7 * t + k], recv_sem=recv_sems.at[7 * t + k],
                device_id=to, device_id_type=MESH)

        started, mine = [], []
        for t, (_, idx) in enumerate(tensors):
            src = srcs[t] if idx is None else srcs[t].at[idx]
            own = pltpu.make_async_copy(src, slot(outs[t], me), local_sems.at[t])
            own.start()
            mine.append(own)
            first = [copy(t, 0, me, sibling, src=src)]
            first += [copy(t, 1 + j, me, (*chip, c), src=src) for j, chip in enumerate(chips)]
            for cp in first:
                cp.start()
            started += first
        for t in range(n):
            for j, chip in enumerate(chips):
                copy(t, 1 + j, (*chip, c), me).wait_recv()
                passed = copy(t, 4 + j, (*chip, c), sibling)
                passed.start()
                started.append(passed)
        for t in range(n):
            copy(t, 0, sibling, me).wait_recv()
            for j, chip in enumerate(chips):
                copy(t, 4 + j, (*chip, 1 - c), me).wait_recv()
        for cp in started:
            cp.wait_send()
        for own in mine:
            own.wait()

    return pl.pallas_call(
        body, name=name,
        in_specs=[ANY] * n, out_specs=[ANY] * n,
        out_shape=[jax.ShapeDtypeStruct((N_DEV,) + tuple(s), a.dtype) for s, (a, _) in zip(shapes, tensors)],
        scratch_shapes=[pltpu.SemaphoreType.DMA((7 * n,)), pltpu.SemaphoreType.DMA((7 * n,)),
                        pltpu.SemaphoreType.DMA((n,))],
    )(*[a for a, _ in tensors])


def _swap_halves(name, grads):
    n = len(grads)

    def body(*refs):
        srcs, got = refs[:n], refs[n:2 * n]
        send_sems, recv_sems = refs[2 * n:]
        x, y, c = _position()
        sends = []
        for t in range(n):
            for k in range(N_XY):
                cp = pltpu.make_async_remote_copy(
                    src_ref=srcs[t].at[2 * k + 1 - c], dst_ref=got[t].at[k],
                    send_sem=send_sems.at[N_XY * t + k], recv_sem=recv_sems.at[N_XY * t + k],
                    device_id=(x, y, 1 - c), device_id_type=MESH)
                cp.start()
                sends.append(cp)
        for cp in sends:
            cp.wait()

    return pl.pallas_call(
        body, name=name, in_specs=[ANY] * n, out_specs=[ANY] * n,
        out_shape=[jax.ShapeDtypeStruct((N_XY,) + g.shape[1:], g.dtype) for g in grads],
        scratch_shapes=[pltpu.SemaphoreType.DMA((N_XY * n,))] * 2,
    )(*grads)


def _chip_partial(name, g, got, core):
    _, R, C = g.shape
    tr = _row_tile(R, C, 4, 2 << 20)

    def body(c_ref, a_ref, b_ref, o_ref):
        o_ref[...] = (a_ref[...] + b_ref[...]).astype(BF16)

    blk = pl.BlockSpec((None, tr, C), lambda k, i, c: (k, i, 0))
    return pl.pallas_call(
        body, name=name,
        grid_spec=pltpu.PrefetchScalarGridSpec(
            num_scalar_prefetch=1, grid=(N_XY, R // tr),
            in_specs=[pl.BlockSpec((None, tr, C), lambda k, i, c: (2 * k + c[0], i, 0)), blk], out_specs=blk),
        out_shape=jax.ShapeDtypeStruct(got.shape, BF16),
        compiler_params=_params(3 * _nbytes((tr, C), F32)),
    )(core, g, got)


def _swap_chips(name, partials):
    n = len(partials)

    def body(*refs):
        srcs, got = refs[:n], refs[n:2 * n]
        send_sems, recv_sems, local_sems = refs[2 * n:]
        x, y, c = _position()
        mine = 2 * x + y
        sends, keeps = [], []
        for t in range(n):
            kp = pltpu.make_async_copy(srcs[t].at[mine], got[t].at[mine], local_sems.at[t])
            kp.start()
            keeps.append(kp)
            for d in range(1, N_XY):
                px, py = x ^ (d >> 1), y ^ (d & 1)
                cp = pltpu.make_async_remote_copy(
                    src_ref=srcs[t].at[2 * px + py], dst_ref=got[t].at[mine],
                    send_sem=send_sems.at[3 * t + d - 1], recv_sem=recv_sems.at[3 * t + d - 1],
                    device_id=(px, py, c), device_id_type=MESH)
                cp.start()
                sends.append(cp)
        for cp in sends:
            cp.wait()
        for kp in keeps:
            kp.wait()

    return pl.pallas_call(
        body, name=name, in_specs=[ANY] * n, out_specs=[ANY] * n,
        out_shape=[jax.ShapeDtypeStruct(p.shape, p.dtype) for p in partials],
        scratch_shapes=[pltpu.SemaphoreType.DMA((3 * n,)), pltpu.SemaphoreType.DMA((3 * n,)),
                        pltpu.SemaphoreType.DMA((n,))],
    )(*partials)


def _adamw(w, g, m, v):
    m = ADAM_B1 * m + (1.0 - ADAM_B1) * g
    v = ADAM_B2 * v + (1.0 - ADAM_B2) * (g * g)
    m_hat = m / (1.0 - ADAM_B1 ** ADAM_STEP)
    v_hat = v / (1.0 - ADAM_B2 ** ADAM_STEP)
    return -ADAM_LR * (m_hat / (jnp.sqrt(v_hat) + ADAM_EPS) + ADAM_WD * w), m, v


def _reduce_update(name, parts, w, m, v, layer, prev):
    _, R, C = parts.shape
    L = w.shape[0]
    tr = _row_tile(R, C, 4, 1 << 20, 8)

    def body(p_ref, w_ref, m_ref, v_ref, *rest):
        g_ref, d_ref, mo_ref, vo_ref = rest[-4:]
        g = p_ref[0].astype(F32)
        for k in range(1, N_XY):
            g = g + p_ref[k].astype(F32)
        g_ref[...] = g
        d_ref[...], mo_ref[...], vo_ref[...] = _adamw(w_ref[...], g, m_ref[...], v_ref[...])

    lay = pl.BlockSpec((None, tr, C), lambda i: (layer, i, 0))
    osh = jax.ShapeDtypeStruct((L, R, C), F32)
    extra = [] if prev is None else list(prev)
    return pl.pallas_call(
        body, name=name, grid=(R // tr,),
        in_specs=[pl.BlockSpec((N_XY, tr, C), lambda i: (0, i, 0)), lay, lay, lay] + [ANY] * len(extra),
        out_specs=[lay] * 4, out_shape=[osh] * 4,
        input_output_aliases={4 + k: k for k in range(len(extra))},
        compiler_params=_params(9 * _nbytes((tr, C), F32)),
    )(parts, w, m, v, *extra)


def _all_reduce_small(name, vec):
    R = vec.shape[0]

    def body(v_ref, o_ref, tot_ref, buf, send_sems, recv_sems):
        x, y, c = _position()
        me = 4 * x + 2 * y + c
        buf[me] = v_ref[...]
        sends = []
        for d in range(1, N_DEV):
            cp = pltpu.make_async_remote_copy(
                src_ref=v_ref, dst_ref=buf.at[me], send_sem=send_sems.at[d - 1], recv_sem=recv_sems.at[d - 1],
                device_id=(x ^ (d >> 2), y ^ ((d >> 1) & 1), c ^ (d & 1)), device_id_type=MESH)
            cp.start()
            sends.append(cp)
        for cp in sends:
            cp.wait()
        s = buf[0]
        for k in range(1, N_DEV):
            s = s + buf[k]
        o_ref[...] = s
        tot_ref[...] = jnp.sum(jnp.sum(s[0:8], axis=0, keepdims=True), axis=1, keepdims=True)

    vm = pl.BlockSpec(memory_space=pltpu.VMEM)
    return pl.pallas_call(
        body, name=name, in_specs=[vm], out_specs=[vm, vm],
        out_shape=[jax.ShapeDtypeStruct(vec.shape, F32), jax.ShapeDtypeStruct((1, 1), F32)],
        scratch_shapes=[pltpu.VMEM((N_DEV, R, LANES), F32), pltpu.SemaphoreType.DMA((N_DEV - 1,)),
                        pltpu.SemaphoreType.DMA((N_DEV - 1,))],
    )(vec)


def _adamw_small(name, w, g, m, v):
    def body(w_ref, g_ref, m_ref, v_ref, d_ref, mo_ref, vo_ref):
        d_ref[...], mo_ref[...], vo_ref[...] = _adamw(w_ref[...], g_ref[...], m_ref[...], v_ref[...])

    vm = pl.BlockSpec(memory_space=pltpu.VMEM)
    return pl.pallas_call(
        body, name=name, in_specs=[vm] * 4, out_specs=[vm] * 3,
        out_shape=[jax.ShapeDtypeStruct(w.shape, F32)] * 3,
    )(w, g, m, v)


def _pack(parts, rows):
    flat = jnp.concatenate([p.reshape(-1) for p in parts])
    return jnp.pad(flat, (0, rows * LANES - flat.shape[0])).reshape(rows, LANES)


def _unpack(packed, shapes, skip=0):
    flat, out, off = packed.reshape(-1), [], skip
    for s in shapes:
        n = 1
        for d in s:
            n *= d
        out.append(flat[off:off + n].reshape(s))
        off += n
    return out


def kernel(x, norm_mix_g, norm_ffn_g, sb_w_qkv, sb_g_q, sb_g_k, sb_w_o, pool_w, pool_scale, conv_w_in, conv_w, conv_w_out, ffn_w_gate, ffn_w_up, ffn_w_down, loss_target, m_norm_mix_g, m_norm_ffn_g, m_sb_w_qkv, m_sb_g_q, m_sb_g_k, m_sb_w_o, m_pool_w, m_pool_scale, m_conv_w_in, m_conv_w, m_conv_w_out, m_ffn_w_gate, m_ffn_w_up, m_ffn_w_down, v_norm_mix_g, v_norm_ffn_g, v_sb_w_qkv, v_sb_g_q, v_sb_g_k, v_sb_w_o, v_pool_w, v_pool_scale, v_conv_w_in, v_conv_w, v_conv_w_out, v_ffn_w_gate, v_ffn_w_up, v_ffn_w_down):
    _, T, D = x.shape
    depth = norm_mix_g.shape[0]
    H = D // HEAD_DIM
    G = len(POOL_WINDOWS)
    pool_rows = pool_w.shape[2]
    xs = x[0]

    big = {
        "sb_w_qkv": (sb_w_qkv, m_sb_w_qkv, v_sb_w_qkv), "sb_w_o": (sb_w_o, m_sb_w_o, v_sb_w_o),
        "pool_w": tuple(a.reshape(a.shape[0], G * pool_rows, a.shape[3]) for a in (pool_w, m_pool_w, v_pool_w)),
        "conv_w_in": (conv_w_in, m_conv_w_in, v_conv_w_in), "conv_w_out": (conv_w_out, m_conv_w_out, v_conv_w_out),
        "ffn_w_gate": (ffn_w_gate, m_ffn_w_gate, v_ffn_w_gate), "ffn_w_up": (ffn_w_up, m_ffn_w_up, v_ffn_w_up),
        "ffn_w_down": (ffn_w_down, m_ffn_w_down, v_ffn_w_down),
    }
    w16 = {k: t[0].astype(BF16) for k, t in big.items()}

    gathered = []
    for i in range(depth):
        kind, j = i % 3, i // 3
        names = [("sb_w_qkv", "sb_w_o"), ("pool_w",), ("conv_w_in", "conv_w_out")][kind]
        tensors = [(w16[nm], j) for nm in names] + [(w16[nm], i) for nm in ("ffn_w_gate", "ffn_w_up", "ffn_w_down")]
        if kind == 2:
            tensors.append((conv_w, j))
        out = _all_gather(f"gather_l{i}", tensors)
        gathered.append(dict(zip(list(names) + ["ffn_w_gate", "ffn_w_up", "ffn_w_down"] + ["conv_w"] * (kind == 2), out)))

    def std_cols(wb):
        return jnp.transpose(wb, (1, 0, 2)).reshape(wb.shape[1], -1)

    saved = []
    xc = xs
    for i in range(depth):
        kind, j = i % 3, i // 3
        gw = gathered[i]
        s = {"x_in": xc}
        if kind == 0:
            h = _rms_fwd(f"norm_mix_l{i}", xc, norm_mix_g[i:i + 1], BF16)
            qkv = _mm_cols(f"qkv_l{i}", h, gw["sb_w_qkv"], F32)
            o32, o16 = _attn_fwd(f"attn_fwd_l{i}", qkv, sb_g_q[j:j + 1], sb_g_k[j:j + 1], H)
            xc = _mm_res(f"attn_out_l{i}", o16, gw["sb_w_o"].reshape(D, D), xc)
            s.update(h=h, qkv=qkv, o32=o32, o16=o16)
        elif kind == 1:
            hf = _rms_fwd(f"norm_mix_l{i}", xc, norm_mix_g[i:i + 1], F32)
            wp = jnp.transpose(gw["pool_w"].reshape(N_DEV, G, pool_rows, D // G), (1, 0, 2, 3)).reshape(G, D // G, D // G)
            xc, p = _pool_fwd(f"pool_fwd_l{i}", hf, xc, wp, pool_scale[j:j + 1])
            s.update(p=p, wp=wp)
        else:
            h = _rms_fwd(f"norm_mix_l{i}", xc, norm_mix_g[i:i + 1], BF16)
            bcx = _mm_cols(f"conv_in_l{i}", h, gw["conv_w_in"], F32)
            cw_full = jnp.transpose(gw["conv_w"], (1, 0, 2)).reshape(3, D)
            by = _conv_fwd(f"conv_fwd_l{i}", bcx, cw_full)
            xc = _mm_res(f"conv_out_l{i}", by, gw["conv_w_out"].reshape(D, D), xc)
            s.update(h=h, bcx=bcx, by=by, cw_full=cw_full)
        s["x_mid"] = xc
        h2 = _rms_fwd(f"norm_ffn_l{i}", xc, norm_ffn_g[i:i + 1], BF16)
        gate, up, act = _ffn_up(f"ffn_up_l{i}", h2, gw["ffn_w_gate"], gw["ffn_w_up"])
        xc = _ffn_down(f"ffn_down_l{i}", act, gw["ffn_w_down"], xc)
        s.update(h2=h2, gate=gate, up=up, act=act)
        saved.append(s)

    dx, dx16, loss_part = _loss_head("loss_head", xc, loss_target[0])

    outs = {k: None for k in big}
    core = lax.axis_index("c").astype(jnp.int32).reshape(1)

    def reduce_layer(i, grads):
        names = list(grads)
        got = _swap_halves(f"swap_core_l{i}", [grads[nm][0] for nm in names])
        partial = [_chip_partial(f"chip_sum_{nm}_l{i}", grads[nm][0], b, core) for nm, b in zip(names, got)]
        parts = _swap_chips(f"swap_chip_l{i}", partial)
        for nm, pt in zip(names, parts):
            w, m, v = big[nm]
            outs[nm] = _reduce_update(f"update_{nm}_l{i}", pt, w, m, v, grads[nm][1], outs[nm])

    d_mix, d_ffn = [None] * depth, [None] * depth
    small = {}
    for i in reversed(range(depth)):
        kind, j = i % 3, i // 3
        gw, s = gathered[i], saved[i]
        grads = {}
        dgate, dup = _ffn_dact(f"ffn_dact_l{i}", dx16, gw["ffn_w_down"], s["gate"], s["up"])
        grads["ffn_w_down"] = (_wgrad_rows(f"ffn_dwd_l{i}", s["act"], dx16), i)
        dh2 = _ffn_dh(f"ffn_dh_l{i}", dgate, dup, gw["ffn_w_gate"], gw["ffn_w_up"])
        dwg, dwu = _wgrad_cols(f"ffn_dwgu_l{i}", s["h2"], [dgate, dup])
        grads["ffn_w_gate"], grads["ffn_w_up"] = (dwg, i), (dwu, i)
        dx, dx16, d_ffn[i] = _rms_bwd(f"norm_ffn_bwd_l{i}", dh2, s["x_mid"], norm_ffn_g[i:i + 1], dx)
        if kind == 0:
            wo = gw["sb_w_o"]
            do16 = _mm_nt_rows(f"attn_do_l{i}", dx16, wo, BF16)
            grads["sb_w_o"] = (_wgrad_std(f"attn_dwo_l{i}", s["o16"], dx16).reshape(wo.shape), j)
            d3, dgq, dgk = _attn_bwd(f"attn_bwd_l{i}", s["qkv"], sb_g_q[j:j + 1], sb_g_k[j:j + 1], do16, s["o32"], H)
            small[("sb_g_q", j)], small[("sb_g_k", j)] = dgq, dgk
            dh = _mm_nt_sections(f"qkv_dh_l{i}", d3, std_cols(gw["sb_w_qkv"]))
            grads["sb_w_qkv"] = (_wgrad_sections(f"qkv_dw_l{i}", s["h"], d3, gw["sb_w_qkv"].shape[2]), j)
        elif kind == 1:
            dh, dwp, dps = _pool_bwd(f"pool_bwd_l{i}", dx, s["p"], s["wp"], pool_scale[j:j + 1])
            small[("pool_scale", j)] = dps
            dwp = jnp.transpose(dwp.reshape(G, N_DEV, pool_rows, D // G), (1, 0, 2, 3)).reshape(N_DEV, G * pool_rows, D // G)
            grads["pool_w"] = (dwp, j)
        else:
            wout = gw["conv_w_out"]
            dby = _mm_nt_rows(f"conv_dby_l{i}", dx16, wout, F32)
            grads["conv_w_out"] = (_wgrad_std(f"conv_dwout_l{i}", s["by"], dx16).reshape(wout.shape), j)
            d3, dcw = _conv_bwd(f"conv_bwd_l{i}", dby, s["bcx"], s["cw_full"])
            small[("conv_w", j)] = dcw[0:3]
            dh = _mm_nt_sections(f"conv_dh_l{i}", d3, std_cols(gw["conv_w_in"]))
            grads["conv_w_in"] = (_wgrad_sections(f"conv_dwin_l{i}", s["h"], d3, gw["conv_w_in"].shape[2]), j)
        dx, dx16, d_mix[i] = _rms_bwd(f"norm_mix_bwd_l{i}", dh, s["x_in"], norm_mix_g[i:i + 1], dx)
        reduce_layer(i, grads)

    n_sb, n_pool, n_conv = sb_g_q.shape[0], pool_scale.shape[0], conv_w.shape[0]
    pieces = [loss_part, jnp.concatenate(d_mix), jnp.concatenate(d_ffn),
              jnp.concatenate([small[("sb_g_q", j)] for j in range(n_sb)]),
              jnp.concatenate([small[("sb_g_k", j)] for j in range(n_sb)]),
              jnp.concatenate([small[("pool_scale", j)] for j in range(n_pool)]),
              jnp.stack([small[("conv_w", j)] for j in range(n_conv)])]
    n_small = sum(p.size for p in pieces)
    rows = -(-n_small // (8 * LANES)) * 8
    summed, loss = _all_reduce_small("reduce_small", _pack(pieces, rows))
    shapes = [norm_mix_g.shape, norm_ffn_g.shape, sb_g_q.shape, sb_g_k.shape, pool_scale.shape, (n_conv, 3, D)]
    g_mix, g_ffn, g_q, g_k, g_ps, g_cw_full = _unpack(summed, shapes, skip=8 * LANES)
    cshard = conv_w.shape[2]
    me = 4 * lax.axis_index("x") + 2 * lax.axis_index("y") + lax.axis_index("c")
    g_cw = lax.dynamic_slice_in_dim(g_cw_full, me * cshard, cshard, axis=2)
    small_names = ["norm_mix_g", "norm_ffn_g", "sb_g_q", "sb_g_k", "pool_scale", "conv_w"]
    small_w = [norm_mix_g, norm_ffn_g, sb_g_q, sb_g_k, pool_scale, conv_w]
    small_m = [m_norm_mix_g, m_norm_ffn_g, m_sb_g_q, m_sb_g_k, m_pool_scale, m_conv_w]
    small_v = [v_norm_mix_g, v_norm_ffn_g, v_sb_g_q, v_sb_g_k, v_pool_scale, v_conv_w]
    small_g = [g_mix, g_ffn, g_q, g_k, g_ps, g_cw]
    n_upd = sum(a.size for a in small_w)
    urows = -(-n_upd // (8 * LANES)) * 8
    sd, sm, sv = _adamw_small("update_small", _pack(small_w, urows), _pack(small_g, urows),
                              _pack(small_m, urows), _pack(small_v, urows))
    sshapes = [a.shape for a in small_w]
    res = {nm: (g, d, m_, v_) for nm, g, d, m_, v_ in
           zip(small_names, small_g, _unpack(sd, sshapes), _unpack(sm, sshapes), _unpack(sv, sshapes))}
    for nm, stacks in outs.items():
        shape = pool_w.shape if nm == "pool_w" else big[nm][0].shape
        res[nm] = tuple(a.reshape(shape) for a in stacks)

    order = ["norm_mix_g", "norm_ffn_g", "sb_w_qkv", "sb_g_q", "sb_g_k", "sb_w_o", "pool_w", "pool_scale",
             "conv_w_in", "conv_w", "conv_w_out", "ffn_w_gate", "ffn_w_up", "ffn_w_down"]
    return (loss.reshape(()), dx[None], *[res[nm][0] for nm in order], *[res[nm][1] for nm in order],
            *[res[nm][2] for nm in order], *[res[nm][3] for nm in order])
```

```python
import functools

import jax
import jax.numpy as jnp
from jax import lax
from jax.experimental import pallas as pl
from jax.experimental.pallas import tpu as pltpu

F32 = jnp.float32
BF16 = jnp.bfloat16

N_DEV = 8
N_XY = 4
HEAD_DIM = 128
LANES = 128
POOL_WINDOWS = (2, 4, 8, 16)
POOL_HALO = 16
CONV_HALO = 8
ATTN_BLOCK = 512
EPS = 1e-6
ADAM_LR = 0.001
ADAM_B1 = 0.9
ADAM_B2 = 0.999
ADAM_EPS = 1e-08
ADAM_WD = 0.01
ADAM_STEP = 10

VMEM_CAP_V7X = 56 * 1024 * 1024
VMEM_FLOOR = 32 * 1024 * 1024

NN = (((1,), (0,)), ((), ()))
NT = (((1,), (1,)), ((), ()))
TN = (((0,), (0,)), ((), ()))
MESH = pl.DeviceIdType.MESH
ANY = pl.BlockSpec(memory_space=pl.ANY)


def _dot(a, b, dims):
    return lax.dot_general(a, b, dims, preferred_element_type=F32)


def _params(step_bytes, scratch_bytes=0):
    need = 2 * step_bytes + scratch_bytes + 3 * step_bytes // 2 + (4 << 20)
    return pltpu.CompilerParams(vmem_limit_bytes=int(min(VMEM_CAP_V7X, max(VMEM_FLOOR, need))))


def _nbytes(shape, dtype):
    n = 1
    for s in shape:
        if s is not None:
            n *= s
    return n * jnp.dtype(dtype).itemsize


def _tile(n, target, mult=8):
    t = min(n, target)
    t -= t % mult
    while t > mult and n % t:
        t -= mult
    assert t >= mult and n % t == 0, (n, target, mult)
    return t


def _row_tile(rows, cols, itemsize, budget, mult=16):
    return _tile(rows, max(mult, budget // (cols * itemsize)), mult)


def _rms_fwd(name, x, g_row, out_dtype, follows=()):
    T, D = x.shape
    tm = _tile(T, 512)

    def body(x_ref, g_ref, *rest):
        o_ref = rest[-1]
        xv = x_ref[...]
        r = lax.rsqrt(jnp.mean(xv * xv, axis=-1, keepdims=True) + EPS)
        o_ref[...] = (xv * r * g_ref[...]).astype(o_ref.dtype)

    return pl.pallas_call(
        body, name=name, grid=(T // tm,),
        in_specs=[pl.BlockSpec((tm, D), lambda i: (i, 0)), pl.BlockSpec((1, D), lambda i: (0, 0))] + [ANY] * len(follows),
        out_specs=pl.BlockSpec((tm, D), lambda i: (i, 0)),
        out_shape=jax.ShapeDtypeStruct((T, D), out_dtype),
        compiler_params=_params(_nbytes((tm, D), F32) * 2),
    )(x, g_row, *follows)


def _rms_bwd(name, dh, x, g_row, dres, follows=()):
    T, D = x.shape
    tm = _tile(T, 256)

    def body(dh_ref, x_ref, g_ref, dres_ref, *rest):
        dx_ref, dx16_ref, dg_ref = rest[-3:]
        xv = x_ref[...]
        r = lax.rsqrt(jnp.mean(xv * xv, axis=-1, keepdims=True) + EPS)
        xh = xv * r
        dhv = dh_ref[...]
        dy = dhv * g_ref[...]
        m = jnp.mean(dy * xh, axis=-1, keepdims=True)
        dx = dres_ref[...] + r * (dy - xh * m)
        dx_ref[...] = dx
        dx16_ref[...] = dx.astype(BF16)

        @pl.when(pl.program_id(0) == 0)
        def _():
            dg_ref[...] = jnp.zeros_like(dg_ref)

        dg_ref[...] += jnp.sum(dhv * xh, axis=0, keepdims=True)

    blk = pl.BlockSpec((tm, D), lambda i: (i, 0))
    row = pl.BlockSpec((1, D), lambda i: (0, 0))
    return pl.pallas_call(
        body, name=name, grid=(T // tm,),
        in_specs=[blk, blk, row, blk] + [ANY] * len(follows), out_specs=[blk, blk, row],
        out_shape=[jax.ShapeDtypeStruct((T, D), F32), jax.ShapeDtypeStruct((T, D), BF16),
                   jax.ShapeDtypeStruct((1, D), F32)],
        compiler_params=_params(_nbytes((tm, D), F32) * 5),
    )(dh, x, g_row, dres, *follows)


def _loss_head(name, y, target):
    T, D = y.shape
    tm = _tile(T, 256)

    def body(y_ref, t_ref, dy_ref, dy16_ref, acc_ref):
        e = y_ref[...] - t_ref[...]
        d = e * (1.0 / D)
        dy_ref[...] = d
        dy16_ref[...] = d.astype(BF16)
        s = (e * e).reshape(tm // 8, 8, D).sum(axis=0)
        part = s[:, 0:LANES]
        for k in range(1, D // LANES):
            part = part + s[:, k * LANES:(k + 1) * LANES]

        @pl.when(pl.program_id(0) == 0)
        def _():
            acc_ref[...] = jnp.zeros_like(acc_ref)

        acc_ref[...] += part * (0.5 / D)

    blk = pl.BlockSpec((tm, D), lambda i: (i, 0))
    return pl.pallas_call(
        body, name=name, grid=(T // tm,),
        in_specs=[blk, blk], out_specs=[blk, blk, pl.BlockSpec((8, LANES), lambda i: (0, 0))],
        out_shape=[jax.ShapeDtypeStruct((T, D), F32), jax.ShapeDtypeStruct((T, D), BF16),
                   jax.ShapeDtypeStruct((8, LANES), F32)],
        compiler_params=_params(_nbytes((tm, D), F32) * 4),
    )(y, target)


def _mm_cols(name, a, w, out_dtype):
    T, K = a.shape
    nb, _, ns = w.shape
    tm = _tile(T, 1024)

    def body(a_ref, w_ref, o_ref):
        o_ref[...] = _dot(a_ref[...], w_ref[...], NN).astype(o_ref.dtype)

    step = _nbytes((tm, K), BF16) + _nbytes((K, ns), BF16) + _nbytes((tm, ns), out_dtype)
    return pl.pallas_call(
        body, name=name, grid=(nb, T // tm),
        in_specs=[pl.BlockSpec((tm, K), lambda b, i: (i, 0)),
                  pl.BlockSpec((None, K, ns), lambda b, i: (b, 0, 0))],
        out_specs=pl.BlockSpec((None, tm, ns), lambda b, i: (b, i, 0)),
        out_shape=jax.ShapeDtypeStruct((nb, T, ns), out_dtype),
        compiler_params=_params(step),
    )(a, w)


def _ffn_up(name, h, wg, wu):
    T, K = h.shape
    nb, _, fs = wg.shape
    tm = _tile(T, 1024)

    def body(a_ref, wg_ref, wu_ref, g_ref, u_ref, act_ref):
        av = a_ref[...]
        g = _dot(av, wg_ref[...], NN)
        u = _dot(av, wu_ref[...], NN)
        g_ref[...] = g
        u_ref[...] = u
        act_ref[...] = (g * jax.nn.sigmoid(g) * u).astype(BF16)

    wspec = pl.BlockSpec((None, K, fs), lambda b, i: (b, 0, 0))
    ospec = pl.BlockSpec((None, tm, fs), lambda b, i: (b, i, 0))
    osh = lambda dt: jax.ShapeDtypeStruct((nb, T, fs), dt)
    step = _nbytes((tm, K), BF16) + 2 * _nbytes((K, fs), BF16) + _nbytes((tm, fs), BF16) + 2 * _nbytes((tm, fs), F32)
    return pl.pallas_call(
        body, name=name, grid=(nb, T // tm),
        in_specs=[pl.BlockSpec((tm, K), lambda b, i: (i, 0)), wspec, wspec],
        out_specs=[ospec, ospec, ospec], out_shape=[osh(F32), osh(F32), osh(BF16)],
        compiler_params=_params(step),
    )(h, wg, wu)


def _mm_res(name, a, w, res):
    T, K = a.shape
    N = w.shape[1]
    tm, tn = _tile(T, 512), _tile(N, 1024, LANES)

    def body(a_ref, w_ref, r_ref, o_ref):
        o_ref[...] = r_ref[...] + _dot(a_ref[...], w_ref[...], NN)

    step = _nbytes((tm, K), BF16) + _nbytes((K, tn), BF16) + 2 * _nbytes((tm, tn), F32)
    return pl.pallas_call(
        body, name=name, grid=(N // tn, T // tm),
        in_specs=[pl.BlockSpec((tm, K), lambda j, i: (i, 0)), pl.BlockSpec((K, tn), lambda j, i: (0, j)),
                  pl.BlockSpec((tm, tn), lambda j, i: (i, j))],
        out_specs=pl.BlockSpec((tm, tn), lambda j, i: (i, j)),
        out_shape=jax.ShapeDtypeStruct((T, N), F32),
        compiler_params=_params(step),
    )(a, w, res)


def _ffn_down(name, act, wd, res):
    nb, T, fs = act.shape
    N = wd.shape[2]
    tm, tn = _tile(T, 1024), _tile(N, 1024, LANES)

    def body(a_ref, w_ref, r_ref, o_ref, acc_ref):
        b = pl.program_id(2)

        @pl.when(b == 0)
        def _():
            acc_ref[...] = r_ref[...]

        acc_ref[...] += _dot(a_ref[...], w_ref[...], NN)

        @pl.when(b == nb - 1)
        def _():
            o_ref[...] = acc_ref[...]

    step = _nbytes((tm, fs), BF16) + _nbytes((fs, tn), BF16) + 2 * _nbytes((tm, tn), F32)
    return pl.pallas_call(
        body, name=name, grid=(T // tm, N // tn, nb),
        in_specs=[pl.BlockSpec((None, tm, fs), lambda i, j, b: (b, i, 0)),
                  pl.BlockSpec((None, fs, tn), lambda i, j, b: (b, 0, j)),
                  pl.BlockSpec((tm, tn), lambda i, j, b: (i, j))],
        out_specs=pl.BlockSpec((tm, tn), lambda i, j, b: (i, j)),
        out_shape=jax.ShapeDtypeStruct((T, N), F32),
        scratch_shapes=[pltpu.VMEM((tm, tn), F32)],
        compiler_params=_params(step, _nbytes((tm, tn), F32)),
    )(act, wd, res)


def _mm_nt_rows(name, g, w, out_dtype):
    T, N = g.shape
    nb, ks, _ = w.shape
    tm = _tile(T, 1024)

    def body(g_ref, w_ref, o_ref):
        o_ref[...] = _dot(g_ref[...], w_ref[...], NT).astype(o_ref.dtype)

    step = _nbytes((tm, N), BF16) + _nbytes((ks, N), BF16) + _nbytes((tm, ks), out_dtype)
    return pl.pallas_call(
        body, name=name, grid=(nb, T // tm),
        in_specs=[pl.BlockSpec((tm, N), lambda b, i: (i, 0)),
                  pl.BlockSpec((None, ks, N), lambda b, i: (b, 0, 0))],
        out_specs=pl.BlockSpec((tm, ks), lambda b, i: (i, b)),
        out_shape=jax.ShapeDtypeStruct((T, nb * ks), out_dtype),
        compiler_params=_params(step),
    )(g, w)


def _ffn_dact(name, dx16, wd, gate, up, follows=()):
    T, N = dx16.shape
    nb, fs, _ = wd.shape
    tm = _tile(T, 1024)

    def body(g_ref, w_ref, gate_ref, up_ref, *rest):
        dg_ref, du_ref = rest[-2:]
        dact = _dot(g_ref[...], w_ref[...], NT)
        gt = gate_ref[...]
        s = jax.nn.sigmoid(gt)
        silu = gt * s
        dg_ref[...] = (dact * up_ref[...] * (s * (1.0 + gt * (1.0 - s)))).astype(BF16)
        du_ref[...] = (dact * silu).astype(BF16)

    aspec = pl.BlockSpec((None, tm, fs), lambda b, i: (b, i, 0))
    osh = jax.ShapeDtypeStruct((nb, T, fs), BF16)
    step = _nbytes((tm, N), BF16) + _nbytes((fs, N), BF16) + 2 * _nbytes((tm, fs), BF16) + 3 * _nbytes((tm, fs), F32)
    return pl.pallas_call(
        body, name=name, grid=(nb, T // tm),
        in_specs=[pl.BlockSpec((tm, N), lambda b, i: (i, 0)),
                  pl.BlockSpec((None, fs, N), lambda b, i: (b, 0, 0)), aspec, aspec] + [ANY] * len(follows),
        out_specs=[aspec, aspec], out_shape=[osh, osh],
        compiler_params=_params(step),
    )(dx16, wd, gate, up, *follows)


def _ffn_dh(name, dgate, dup, wg, wu):
    nb, T, fs = dgate.shape
    D = wg.shape[1]
    tm, tn = _tile(T, 1024), _tile(D, 1024, LANES)

    def body(dg_ref, du_ref, wg_ref, wu_ref, o_ref, acc_ref):
        b = pl.program_id(2)

        @pl.when(b == 0)
        def _():
            acc_ref[...] = jnp.zeros_like(acc_ref)

        acc_ref[...] += _dot(dg_ref[...], wg_ref[...], NT) + _dot(du_ref[...], wu_ref[...], NT)

        @pl.when(b == nb - 1)
        def _():
            o_ref[...] = acc_ref[...]

    aspec = pl.BlockSpec((None, tm, fs), lambda i, j, b: (b, i, 0))
    wspec = pl.BlockSpec((None, tn, fs), lambda i, j, b: (b, j, 0))
    step = 2 * _nbytes((tm, fs), BF16) + 2 * _nbytes((tn, fs), BF16) + _nbytes((tm, tn), F32)
    return pl.pallas_call(
        body, name=name, grid=(T // tm, D // tn, nb),
        in_specs=[aspec, aspec, wspec, wspec],
        out_specs=pl.BlockSpec((tm, tn), lambda i, j, b: (i, j)),
        out_shape=jax.ShapeDtypeStruct((T, D), F32),
        scratch_shapes=[pltpu.VMEM((tm, tn), F32)],
        compiler_params=_params(step, _nbytes((tm, tn), F32)),
    )(dgate, dup, wg, wu)


def _mm_nt_sections(name, g3, w_std):
    ns_, T, Ds = g3.shape
    D = w_std.shape[0]
    tm, tn, tk = _tile(T, 1024), _tile(D, 1024, LANES), _tile(Ds, 1024, LANES)
    nkk = Ds // tk
    nk = ns_ * nkk

    def body(g_ref, w_ref, o_ref, acc_ref):
        r = pl.program_id(2)

        @pl.when(r == 0)
        def _():
            acc_ref[...] = jnp.zeros_like(acc_ref)

        acc_ref[...] += _dot(g_ref[...], w_ref[...], NT)

        @pl.when(r == nk - 1)
        def _():
            o_ref[...] = acc_ref[...]

    step = _nbytes((tm, tk), BF16) + _nbytes((tn, tk), BF16) + _nbytes((tm, tn), F32)
    return pl.pallas_call(
        body, name=name, grid=(T // tm, D // tn, nk),
        in_specs=[pl.BlockSpec((None, tm, tk), lambda i, j, r: (r // nkk, i, r % nkk)),
                  pl.BlockSpec((tn, tk), lambda i, j, r: (j, r))],
        out_specs=pl.BlockSpec((tm, tn), lambda i, j, r: (i, j)),
        out_shape=jax.ShapeDtypeStruct((T, D), F32),
        scratch_shapes=[pltpu.VMEM((tm, tn), F32)],
        compiler_params=_params(step, _nbytes((tm, tn), F32)),
    )(g3, w_std)


def _wgrad(name, grid, a, a_spec, gs, g_spec, out_shape, o_spec, acc_shape):
    n = len(gs)
    nt = grid[-1]

    def body(*refs):
        a_ref, g_refs, o_refs, acc_refs = refs[0], refs[1:1 + n], refs[1 + n:1 + 2 * n], refs[1 + 2 * n:]
        t = pl.program_id(len(grid) - 1)
        av = a_ref[...]
        for g_ref, o_ref, acc_ref in zip(g_refs, o_refs, acc_refs):
            @pl.when(t == 0)
            def _():
                acc_ref[...] = jnp.zeros_like(acc_ref)

            acc_ref[...] += _dot(av, g_ref[...], TN)

            @pl.when(t == nt - 1)
            def _():
                o_ref[...] = acc_ref[...]

    step = _nbytes(a_spec.block_shape, BF16) + n * (_nbytes(g_spec.block_shape, BF16) + _nbytes(acc_shape, F32))
    outs = pl.pallas_call(
        body, name=name, grid=grid,
        in_specs=[a_spec] + [g_spec] * n, out_specs=[o_spec] * n,
        out_shape=[jax.ShapeDtypeStruct(out_shape, F32)] * n,
        scratch_shapes=[pltpu.VMEM(acc_shape, F32)] * n,
        compiler_params=_params(step, n * _nbytes(acc_shape, F32)),
    )(a, *gs)
    return outs


def _wgrad_cols(name, a, gs):
    T, K = a.shape
    nb, _, ns = gs[0].shape
    tk, tt = _tile(K, 1024, LANES), _tile(T, 1024)
    return _wgrad(name, (nb, K // tk, T // tt), a,
                  pl.BlockSpec((tt, tk), lambda b, k, t: (t, k)), gs,
                  pl.BlockSpec((None, tt, ns), lambda b, k, t: (b, t, 0)),
                  (nb, K, ns), pl.BlockSpec((None, tk, ns), lambda b, k, t: (b, k, 0)), (tk, ns))


def _wgrad_rows(name, act, dx16):
    nb, T, fs = act.shape
    N = dx16.shape[1]
    tn, tt = _tile(N, 1024, LANES), _tile(T, 1024)
    return _wgrad(name, (nb, N // tn, T // tt), act,
                  pl.BlockSpec((None, tt, fs), lambda b, j, t: (b, t, 0)), [dx16],
                  pl.BlockSpec((tt, tn), lambda b, j, t: (t, j)),
                  (nb, fs, N), pl.BlockSpec((None, fs, tn), lambda b, j, t: (b, 0, j)), (fs, tn))[0]


def _wgrad_std(name, a, g):
    T, K = a.shape
    N = g.shape[1]
    tk, tn, tt = _tile(K, 1024, LANES), _tile(N, 1024, LANES), _tile(T, 1024)
    return _wgrad(name, (K // tk, N // tn, T // tt), a,
                  pl.BlockSpec((tt, tk), lambda k, j, t: (t, k)), [g],
                  pl.BlockSpec((tt, tn), lambda k, j, t: (t, j)),
                  (K, N), pl.BlockSpec((tk, tn), lambda k, j, t: (k, j)), (tk, tn))[0]


def _wgrad_sections(name, a, g3, ns):
    T, K = a.shape
    nsec, _, Ds = g3.shape
    cw = 256 if (ns % 256 == 0 and Ds % 256 == 0) else LANES
    per_sec, per_shard = Ds // cw, ns // cw
    nb = nsec * Ds // ns
    tk, tt = _tile(K, 1024, LANES), _tile(T, 1024)
    return _wgrad(name, (nsec * per_sec, K // tk, T // tt), a,
                  pl.BlockSpec((tt, tk), lambda p, k, t: (t, k)), [g3],
                  pl.BlockSpec((None, tt, cw), lambda p, k, t: (p // per_sec, t, p % per_sec)),
                  (nb, K, ns), pl.BlockSpec((None, tk, cw), lambda p, k, t: (p // per_shard, k, p % per_shard)),
                  (tk, cw))[0]


def _tri2():
    r = lax.broadcasted_iota(jnp.int32, (LANES, 2 * LANES), 0)
    c = lax.broadcasted_iota(jnp.int32, (LANES, 2 * LANES), 1)
    return jnp.where((r >= c) | (c >= LANES), 1.0, 0.0).astype(BF16)


def _suffix_sums(v, tri):
    hi = v.astype(BF16)
    lo = (v - hi.astype(F32)).astype(BF16)
    both = _dot(hi, tri, NN) + _dot(lo, tri, NN)
    return both[:, :LANES], both[:, LANES:]


def _suffix_sums_wide(v, carry, tri):
    pieces = [None] * (v.shape[1] // LANES)
    for u in reversed(range(len(pieces))):
        incl, tot = _suffix_sums(v[:, u * LANES:(u + 1) * LANES], tri)
        pieces[u] = incl + carry
        carry = carry + tot
    return jnp.concatenate(pieces, axis=1), carry


def _sb_group(z, mask, carry, tri):
    lsn = -(jnp.maximum(z, 0.0) + jnp.log(1.0 + jnp.exp(-jnp.abs(z))))
    lk = lsn if mask is None else jnp.where(mask, lsn, 0.0)
    incl, carry = _suffix_sums_wide(lk, carry, tri)
    a = jnp.exp(z + lsn + (incl - lk))
    return (a if mask is None else jnp.where(mask, a, 0.0)), lsn, carry


def _head_norm_store(src_ref, g_ref, dst_ref, rows, chunk):
    def step(i, _):
        r0 = pl.multiple_of(i * chunk, chunk)
        v = src_ref[pl.ds(r0, chunk), :]
        r = lax.rsqrt(jnp.mean(v * v, axis=-1, keepdims=True) + EPS)
        dst_ref[pl.ds(r0, chunk), :] = (v * r * g_ref[...]).astype(dst_ref.dtype)
        return 0

    lax.fori_loop(0, rows // chunk, step, 0)


def _qkv_specs(T, ns, H):
    cps = ns // HEAD_DIM

    def spec(sec):
        return pl.BlockSpec((None, T, HEAD_DIM), lambda h: ((sec * H + h) // cps, 0, (sec * H + h) % cps))

    return [spec(0), spec(1), spec(2)]


def _attn_fwd(name, qkv, gq, gk, H):
    _, T, ns = qkv.shape
    D = H * HEAD_DIM
    BQ = _tile(T, ATTN_BLOCK, LANES)
    scale = HEAD_DIM ** -0.5

    def body(q_ref, k_ref, v_ref, gq_ref, gk_ref, o32_ref, o16_ref, qn, kn, vb):
        _head_norm_store(q_ref, gq_ref, qn, T, BQ)
        _head_norm_store(k_ref, gk_ref, kn, T, BQ)
        vb[...] = v_ref[...].astype(BF16)
        tri = _tri2()
        causal = lax.broadcasted_iota(jnp.int32, (BQ, BQ), 1) < lax.broadcasted_iota(jnp.int32, (BQ, BQ), 0)

        def qloop(qi, _):
            t0 = pl.multiple_of(qi * BQ, BQ)
            qb = qn[pl.ds(t0, BQ), :]

            def step(grp, carry, mask):
                o_acc, o_low, cr = carry
                s0 = pl.multiple_of(grp * BQ, BQ)
                z = _dot(qb, kn[pl.ds(s0, BQ), :], NT) * scale
                a, _, cr = _sb_group(z, mask, cr, tri)
                vj = vb[pl.ds(s0, BQ), :]
                a_hi = a.astype(BF16)
                o_acc = o_acc + _dot(a_hi, vj, NN)
                o_low = o_low + _dot((a - a_hi.astype(F32)).astype(BF16), vj, NN)
                return o_acc, o_low, cr

            zero = jnp.zeros((BQ, LANES), F32)
            first = step(qi, (zero, zero, zero), causal)
            o_acc, o_low, _ = lax.fori_loop(0, qi, lambda jj, c: step(qi - 1 - jj, c, None), first)
            o32_ref[pl.ds(t0, BQ), :] = o_acc + o_low
            o16_ref[pl.ds(t0, BQ), :] = o_acc.astype(BF16)
            return 0

        lax.fori_loop(0, T // BQ, qloop, 0)

    gspec = pl.BlockSpec((1, HEAD_DIM), lambda h: (0, 0))
    ospec = pl.BlockSpec((T, HEAD_DIM), lambda h: (0, h))
    step = 3 * _nbytes((T, HEAD_DIM), F32) + _nbytes((T, HEAD_DIM), F32) + _nbytes((T, HEAD_DIM), BF16)
    return pl.pallas_call(
        body, name=name, grid=(H,),
        in_specs=_qkv_specs(T, ns, H) + [gspec, gspec],
        out_specs=[ospec, ospec],
        out_shape=[jax.ShapeDtypeStruct((T, D), F32), jax.ShapeDtypeStruct((T, D), BF16)],
        scratch_shapes=[pltpu.VMEM((T, HEAD_DIM), BF16)] * 3,
        compiler_params=_params(step, 3 * _nbytes((T, HEAD_DIM), BF16) + 10 * _nbytes((BQ, BQ), F32)),
    )(qkv, qkv, qkv, gq, gk)


def _attn_bwd(name, qkv, gq, gk, do16, o32, H):
    _, T, ns = qkv.shape
    D = H * HEAD_DIM
    BQ = _tile(T, ATTN_BLOCK, LANES)
    scale = HEAD_DIM ** -0.5

    def body(q_ref, k_ref, v_ref, gq_ref, gk_ref, do_ref, o_ref, d3_ref, dgq_ref, dgk_ref,
             qn, kn, vb, dqn, dkn, dv):
        h = pl.program_id(0)
        _head_norm_store(q_ref, gq_ref, qn, T, BQ)
        _head_norm_store(k_ref, gk_ref, kn, T, BQ)
        vb[...] = v_ref[...].astype(BF16)
        dkn[...] = jnp.zeros_like(dkn)
        dv[...] = jnp.zeros_like(dv)
        tri = _tri2()
        causal = lax.broadcasted_iota(jnp.int32, (BQ, BQ), 1) < lax.broadcasted_iota(jnp.int32, (BQ, BQ), 0)

        def qloop(qi, _):
            t0 = pl.multiple_of(qi * BQ, BQ)
            qb = qn[pl.ds(t0, BQ), :]
            dob = do_ref[pl.ds(t0, BQ), :]
            total = jnp.sum(dob.astype(F32) * o_ref[pl.ds(t0, BQ), :], axis=-1, keepdims=True)

            def step(grp, carry, mask):
                dq_acc, cr, crd = carry
                s0 = pl.multiple_of(grp * BQ, BQ)
                kb = kn[pl.ds(s0, BQ), :]
                vj = vb[pl.ds(s0, BQ), :]
                z = _dot(qb, kb, NT) * scale
                a, lsn, cr = _sb_group(z, mask, cr, tri)
                dla = a * _dot(dob, vj, NT)
                later, crd = _suffix_sums_wide(dla, crd, tri)
                sig = jnp.exp(z + lsn)
                d_keep = (total - later) * sig
                dz = (dla * (1.0 - sig) - (d_keep if mask is None else jnp.where(mask, d_keep, 0.0))) * scale
                dzb = dz.astype(BF16)
                dq_acc = dq_acc + _dot(dzb, kb, NN)
                dkn[pl.ds(s0, BQ), :] += _dot(dzb, qb, TN)
                dv[pl.ds(s0, BQ), :] += _dot(a.astype(BF16), dob, TN)
                return dq_acc, cr, crd

            zero = jnp.zeros((BQ, LANES), F32)
            first = step(qi, (zero, zero, zero), causal)
            dq_acc, _, _ = lax.fori_loop(0, qi, lambda jj, c: step(qi - 1 - jj, c, None), first)
            dqn[pl.ds(t0, BQ), :] = dq_acc
            return 0

        lax.fori_loop(0, T // BQ, qloop, 0)

        @pl.when(h == 0)
        def _():
            dgq_ref[...] = jnp.zeros_like(dgq_ref)
            dgk_ref[...] = jnp.zeros_like(dgk_ref)

        def norm_bwd(src_ref, g_ref, dy_ref, sec, dg_ref):
            def step(i, _):
                r0 = pl.multiple_of(i * BQ, BQ)
                v = src_ref[pl.ds(r0, BQ), :]
                r = lax.rsqrt(jnp.mean(v * v, axis=-1, keepdims=True) + EPS)
                vh = v * r
                dyo = dy_ref[pl.ds(r0, BQ), :]
                dy = dyo * g_ref[...]
                m = jnp.mean(dy * vh, axis=-1, keepdims=True)
                d3_ref[sec, pl.ds(r0, BQ), :] = (r * (dy - vh * m)).astype(BF16)
                dg_ref[...] += jnp.sum(dyo * vh, axis=0, keepdims=True)
                return 0

            lax.fori_loop(0, T // BQ, step, 0)

        norm_bwd(q_ref, gq_ref, dqn, 0, dgq_ref)
        norm_bwd(k_ref, gk_ref, dkn, 1, dgk_ref)
        d3_ref[2] = dv[...].astype(BF16)

    gspec = pl.BlockSpec((1, HEAD_DIM), lambda h: (0, 0))
    hspec = pl.BlockSpec((T, HEAD_DIM), lambda h: (0, h))
    step = (3 * _nbytes((T, HEAD_DIM), F32) + _nbytes((T, HEAD_DIM), BF16) + _nbytes((T, HEAD_DIM), F32)
            + 3 * _nbytes((T, HEAD_DIM), BF16))
    scratch = 3 * _nbytes((T, HEAD_DIM), BF16) + 3 * _nbytes((T, HEAD_DIM), F32)
    return pl.pallas_call(
        body, name=name, grid=(H,),
        in_specs=_qkv_specs(T, ns, H) + [gspec, gspec, hspec, hspec],
        out_specs=[pl.BlockSpec((3, T, HEAD_DIM), lambda h: (0, 0, h)), gspec, gspec],
        out_shape=[jax.ShapeDtypeStruct((3, T, D), BF16), jax.ShapeDtypeStruct((1, HEAD_DIM), F32),
                   jax.ShapeDtypeStruct((1, HEAD_DIM), F32)],
        scratch_shapes=[pltpu.VMEM((T, HEAD_DIM), BF16)] * 3 + [pltpu.VMEM((T, HEAD_DIM), F32)] * 3,
        compiler_params=_params(step, scratch + 12 * _nbytes((BQ, BQ), F32)),
    )(qkv, qkv, qkv, gq, gk, do16, o32)


def _by_group(g, vals):
    out = vals[-1]
    for k in range(len(vals) - 2, -1, -1):
        out = jnp.where(g == k, vals[k], out)
    return out


def _pool_fwd(name, hf, x, wp, scale_row):
    T, D = x.shape
    G = len(POOL_WINDOWS)
    C = D // G
    tm = _tile(T, 512, POOL_HALO)
    hb = tm // POOL_HALO

    def body(h_ref, halo_ref, x_ref, w_ref, s_ref, xo_ref, p_ref):
        g, i = pl.program_id(0), pl.program_id(1)
        hv = h_ref[...]
        ext = jnp.concatenate([halo_ref[...] * (i > 0).astype(F32), hv], axis=0)
        sums, acc = [], ext
        for k in range(len(POOL_WINDOWS)):
            acc = acc + pltpu.roll(acc, 1 << k, 0)
            sums.append(acc)
        ws = _by_group(g, sums)[POOL_HALO:]
        t = i * tm + lax.broadcasted_iota(jnp.int32, (tm, 1), 0)
        cnt = jnp.minimum(t + 1, lax.shift_left(jnp.int32(2), g)).astype(F32)
        p = (ws / cnt - hv).astype(BF16)
        p_ref[...] = p
        xo_ref[...] = x_ref[...] + _dot(p, w_ref[...], NN) * s_ref[...]

    blk = pl.BlockSpec((tm, C), lambda g, i: (i, g))
    step = 3 * _nbytes((tm, C), F32) + _nbytes((tm, C), BF16) + _nbytes((C, C), BF16)
    return pl.pallas_call(
        body, name=name, grid=(G, T // tm),
        in_specs=[blk, pl.BlockSpec((POOL_HALO, C), lambda g, i: (jnp.maximum(i * hb - 1, 0), g)), blk,
                  pl.BlockSpec((None, C, C), lambda g, i: (g, 0, 0)), pl.BlockSpec((1, C), lambda g, i: (0, g))],
        out_specs=[blk, blk],
        out_shape=[jax.ShapeDtypeStruct((T, D), F32), jax.ShapeDtypeStruct((T, D), BF16)],
        compiler_params=_params(step + 6 * _nbytes((tm, C), F32)),
    )(hf, hf, x, wp, scale_row)


def _pool_bwd(name, dx, p, wp, scale_row):
    T, D = dx.shape
    G = len(POOL_WINDOWS)
    C = D // G
    tm = _tile(T, 512, POOL_HALO)
    hb = tm // POOL_HALO
    nt = T // tm
    n = tm + POOL_HALO

    def body(dx_ref, halo_ref, p_ref, w_ref, s_ref, dh_ref, dw_ref, ds_ref):
        g, i = pl.program_id(0), pl.program_id(1)
        dxv = dx_ref[...]
        dxe = jnp.concatenate([dxv, halo_ref[...] * (i < nt - 1).astype(F32)], axis=0)
        dyp = (dxe * s_ref[...]).astype(BF16)
        wv = w_ref[...]
        dp = _dot(dyp, wv, NT)
        t = i * tm + lax.broadcasted_iota(jnp.int32, (n, 1), 0)
        cnt = jnp.minimum(t + 1, lax.shift_left(jnp.int32(2), g)).astype(F32)
        sums, acc = [], dp / cnt
        for k in range(len(POOL_WINDOWS)):
            acc = acc + pltpu.roll(acc, n - (1 << k), 0)
            sums.append(acc)
        dh_ref[...] = _by_group(g, sums)[:tm] - dp[:tm]
        pv = p_ref[...]

        @pl.when(i == 0)
        def _():
            dw_ref[...] = jnp.zeros_like(dw_ref)
            ds_ref[...] = jnp.zeros_like(ds_ref)

        ds_ref[...] += jnp.sum(dxv * _dot(pv, wv, NN), axis=0, keepdims=True)
        dw_ref[...] += _dot(pv, dyp[:tm], TN)

    blk = pl.BlockSpec((tm, C), lambda g, i: (i, g))
    step = 2 * _nbytes((tm, C), F32) + _nbytes((tm, C), BF16) + _nbytes((C, C), BF16) + _nbytes((C, C), F32)
    return pl.pallas_call(
        body, name=name, grid=(G, nt),
        in_specs=[blk, pl.BlockSpec((POOL_HALO, C), lambda g, i: (jnp.minimum((i + 1) * hb, T // POOL_HALO - 1), g)),
                  blk, pl.BlockSpec((None, C, C), lambda g, i: (g, 0, 0)), pl.BlockSpec((1, C), lambda g, i: (0, g))],
        out_specs=[blk, pl.BlockSpec((None, C, C), lambda g, i: (g, 0, 0)), pl.BlockSpec((1, C), lambda g, i: (0, g))],
        out_shape=[jax.ShapeDtypeStruct((T, D), F32), jax.ShapeDtypeStruct((G, C, C), F32),
                   jax.ShapeDtypeStruct((1, D), F32)],
        compiler_params=_params(step + 8 * _nbytes((tm, C), F32)),
    )(dx, dx, p, wp, scale_row)


def _section_spec(rows, cw, ns, D, sec, row_map):
    per = ns // cw

    def imap(j, i):
        c = (sec * D) // cw + j
        return (c // per, row_map(i), c % per)

    return pl.BlockSpec((None, rows, cw), imap)


def _conv_fwd(name, bcx, cw_full):
    _, T, ns = bcx.shape
    D = cw_full.shape[1]
    cw = 256 if (ns % 256 == 0 and D % 256 == 0) else LANES
    tm = _tile(T, 512, CONV_HALO)
    hb = tm // CONV_HALO

    def body(b_ref, c_ref, u_ref, ch_ref, uh_ref, w_ref, o_ref):
        i = pl.program_id(1)
        gm = c_ref[...] * u_ref[...]
        ext = jnp.concatenate([ch_ref[...] * uh_ref[...] * (i > 0).astype(F32), gm], axis=0)
        w0, w1, w2 = w_ref[0:1, :], w_ref[1:2, :], w_ref[2:3, :]
        y = w2 * gm + w1 * pltpu.roll(ext, 1, 0)[CONV_HALO:] + w0 * pltpu.roll(ext, 2, 0)[CONV_HALO:]
        o_ref[...] = (b_ref[...] * y).astype(BF16)

    main = lambda i: i
    prev = lambda i: jnp.maximum(i * hb - 1, 0)
    return pl.pallas_call(
        body, name=name, grid=(D // cw, T // tm),
        in_specs=[_section_spec(tm, cw, ns, D, 0, main), _section_spec(tm, cw, ns, D, 1, main),
                  _section_spec(tm, cw, ns, D, 2, main), _section_spec(CONV_HALO, cw, ns, D, 1, prev),
                  _section_spec(CONV_HALO, cw, ns, D, 2, prev), pl.BlockSpec((3, cw), lambda j, i: (0, j))],
        out_specs=pl.BlockSpec((tm, cw), lambda j, i: (i, j)),
        out_shape=jax.ShapeDtypeStruct((T, D), BF16),
        compiler_params=_params(8 * _nbytes((tm, cw), F32)),
    )(bcx, bcx, bcx, bcx, bcx, cw_full)


def _conv_bwd(name, dby, bcx, cw_full):
    _, T, ns = bcx.shape
    D = cw_full.shape[1]
    cw = 256 if (ns % 256 == 0 and D % 256 == 0) else LANES
    tm = _tile(T, 512, CONV_HALO)
    hb = tm // CONV_HALO
    nt = T // tm
    n = tm + CONV_HALO

    def body(dby_ref, dbyh_ref, b_ref, bh_ref, c_ref, u_ref, ch_ref, uh_ref, w_ref, d3_ref, dw_ref):
        i = pl.program_id(1)
        w0, w1, w2 = w_ref[0:1, :], w_ref[1:2, :], w_ref[2:3, :]
        bv, cv, uv, dbyv = b_ref[...], c_ref[...], u_ref[...], dby_ref[...]
        gm = cv * uv
        ext_g = jnp.concatenate([ch_ref[...] * uh_ref[...] * (i > 0).astype(F32), gm], axis=0)
        g1 = pltpu.roll(ext_g, 1, 0)[CONV_HALO:]
        g2 = pltpu.roll(ext_g, 2, 0)[CONV_HALO:]
        y = w2 * gm + w1 * g1 + w0 * g2
        dy = dbyv * bv
        ext_dy = jnp.concatenate([dy, dbyh_ref[...] * bh_ref[...] * (i < nt - 1).astype(F32)], axis=0)
        dg = w2 * dy + w1 * pltpu.roll(ext_dy, n - 1, 0)[:tm] + w0 * pltpu.roll(ext_dy, n - 2, 0)[:tm]
        d3_ref[0] = (dbyv * y).astype(BF16)
        d3_ref[1] = (dg * uv).astype(BF16)
        d3_ref[2] = (dg * cv).astype(BF16)

        @pl.when(i == 0)
        def _():
            dw_ref[...] = jnp.zeros_like(dw_ref)

        rows = [jnp.sum(dy * v, axis=0, keepdims=True) for v in (g2, g1, gm)]
        dw_ref[...] += jnp.concatenate(rows + [jnp.zeros((8 - len(rows), cw), F32)], axis=0)

    main = lambda i: i
    prev = lambda i: jnp.maximum(i * hb - 1, 0)
    nxt = lambda i: jnp.minimum((i + 1) * hb, T // CONV_HALO - 1)
    return pl.pallas_call(
        body, name=name, grid=(D // cw, nt),
        in_specs=[pl.BlockSpec((tm, cw), lambda j, i: (i, j)), pl.BlockSpec((CONV_HALO, cw), lambda j, i: (nxt(i), j)),
                  _section_spec(tm, cw, ns, D, 0, main), _section_spec(CONV_HALO, cw, ns, D, 0, nxt),
                  _section_spec(tm, cw, ns, D, 1, main), _section_spec(tm, cw, ns, D, 2, main),
                  _section_spec(CONV_HALO, cw, ns, D, 1, prev), _section_spec(CONV_HALO, cw, ns, D, 2, prev),
                  pl.BlockSpec((3, cw), lambda j, i: (0, j))],
        out_specs=[pl.BlockSpec((3, tm, cw), lambda j, i: (0, i, j)), pl.BlockSpec((8, cw), lambda j, i: (0, j))],
        out_shape=[jax.ShapeDtypeStruct((3, T, D), BF16), jax.ShapeDtypeStruct((8, D), F32)],
        compiler_params=_params(14 * _nbytes((tm, cw), F32)),
    )(dby, dby, bcx, bcx, bcx, bcx, bcx, bcx, cw_full)


def _position():
    return lax.axis_index("x"), lax.axis_index("y"), lax.axis_index("c")


def _all_gather(name, tensors):
    n = len(tensors)
    shapes = [a.shape[1:] if idx is not None else a.shape for a, idx in tensors]

    def body(*refs):
        srcs, outs = refs[:n], refs[n:2 * n]
        send_sems, recv_sems, local_sems = refs[2 * n:]
        x, y, c = _position()
        me, sibling = (x, y, c), (x, y, 1 - c)
        chips = [(1 - x, y), (x, 1 - y), (1 - x, 1 - y)]

        def slot(out, p):
            return out.at[4 * p[0] + 2 * p[1] + p[2]]

        def copy(t, k, block, to, src=None):
            return pltpu.make_async_remote_copy(
                src_ref=slot(outs[t], block) if src is None else src, dst_ref=slot(outs[t], block),
                send_sem=send_sems.at[7 * t + k], recv_sem=recv_sems.at[7 * t + k],
                device_id=to, device_id_type=MESH)

        started, mine = [], []
        for t, (_, idx) in enumerate(tensors):
            src = srcs[t] if idx is None else srcs[t].at[idx]
            own = pltpu.make_async_copy(src, slot(outs[t], me), local_sems.at[t])
            own.start()
            mine.append(own)
            first = [copy(t, 0, me, sibling, src=src)]
            first += [copy(t, 1 + j, me, (*chip, c), src=src) for j, chip in enumerate(chips)]
            for cp in first:
                cp.start()
            started += first
        for t in range(n):
            for j, chip in enumerate(chips):
                copy(t, 1 + j, (*chip, c), me).wait_recv()
                passed = copy(t, 4 + j, (*chip, c), sibling)
                passed.start()
                started.append(passed)
        for t in range(n):
            copy(t, 0, sibling, me).wait_recv()
            for j, chip in enumerate(chips):
                copy(t, 4 + j, (*chip, 1 - c), me).wait_recv()
        for cp in started:
            cp.wait_send()
        for own in mine:
            own.wait()

    return pl.pallas_call(
        body, name=name,
        in_specs=[ANY] * n, out_specs=[ANY] * n,
        out_shape=[jax.ShapeDtypeStruct((N_DEV,) + tuple(s), a.dtype) for s, (a, _) in zip(shapes, tensors)],
        scratch_shapes=[pltpu.SemaphoreType.DMA((7 * n,)), pltpu.SemaphoreType.DMA((7 * n,)),
                        pltpu.SemaphoreType.DMA((n,))],
    )(*[a for a, _ in tensors])


def _swap_halves(name, grads):
    n = len(grads)

    def body(*refs):
        srcs, got = refs[:n], refs[n:2 * n]
        send_sems, recv_sems = refs[2 * n:]
        x, y, c = _position()
        sends = []
        for t in range(n):
            for k in range(N_XY):
                cp = pltpu.make_async_remote_copy(
                    src_ref=srcs[t].at[2 * k + 1 - c], dst_ref=got[t].at[k],
                    send_sem=send_sems.at[N_XY * t + k], recv_sem=recv_sems.at[N_XY * t + k],
                    device_id=(x, y, 1 - c), device_id_type=MESH)
                cp.start()
                sends.append(cp)
        for cp in sends:
            cp.wait()

    return pl.pallas_call(
        body, name=name, in_specs=[ANY] * n, out_specs=[ANY] * n,
        out_shape=[jax.ShapeDtypeStruct((N_XY,) + g.shape[1:], g.dtype) for g in grads],
        scratch_shapes=[pltpu.SemaphoreType.DMA((N_XY * n,))] * 2,
    )(*grads)


def _chip_partial(name, g, got, core):
    _, R, C = g.shape
    tr = _row_tile(R, C, 4, 2 << 20)

    def body(c_ref, a_ref, b_ref, o_ref):
        o_ref[...] = (a_ref[...] + b_ref[...]).astype(BF16)

    blk = pl.BlockSpec((None, tr, C), lambda k, i, c: (k, i, 0))
    return pl.pallas_call(
        body, name=name,
        grid_spec=pltpu.PrefetchScalarGridSpec(
            num_scalar_prefetch=1, grid=(N_XY, R // tr),
            in_specs=[pl.BlockSpec((None, tr, C), lambda k, i, c: (2 * k + c[0], i, 0)), blk], out_specs=blk),
        out_shape=jax.ShapeDtypeStruct(got.shape, BF16),
        compiler_params=_params(3 * _nbytes((tr, C), F32)),
    )(core, g, got)


def _chip_copies(n):
    def copies(srcs, lands, send_sems, recv_sems):
        x, y, c = _position()
        mine = 2 * x + y
        out = []
        for t in range(n):
            for d in range(1, N_XY):
                px, py = x ^ (d >> 1), y ^ (d & 1)
                out.append(pltpu.make_async_remote_copy(
                    src_ref=srcs[t].at[2 * px + py], dst_ref=lands[t].at[mine],
                    send_sem=send_sems.at[3 * t + d - 1], recv_sem=recv_sems.at[3 * t + d - 1],
                    device_id=(px, py, c), device_id_type=MESH))
        return out

    return copies, 3 * n


def _gather_copies(tensors):
    def copies(srcs, lands, send_sems, recv_sems):
        x, y, c = _position()
        me = 4 * x + 2 * y + c
        out = []
        for t, (_, idx) in enumerate(tensors):
            for d in range(1, N_DEV):
                out.append(pltpu.make_async_remote_copy(
                    src_ref=srcs[t].at[idx], dst_ref=lands[t].at[me],
                    send_sem=send_sems.at[7 * t + d - 1], recv_sem=recv_sems.at[7 * t + d - 1],
                    device_id=(x ^ (d >> 2), y ^ ((d >> 1) & 1), c ^ (d & 1)), device_id_type=MESH))
        return out

    return copies, 7 * len(tensors)


HBM = pl.BlockSpec(memory_space=pltpu.HBM)
SEM = pl.BlockSpec(memory_space=pltpu.SEMAPHORE)
EFFECT = pltpu.SideEffectType.DATAFLOW_SIDE_EFFECTING


def _exchange_start(name, exchange, srcs, lands, after):
    copies, n_sem = exchange
    n = len(srcs)

    def body(*refs):
        for cp in copies(refs[:n], refs[n:2 * n], refs[2 * n + 1], refs[2 * n + 2]):
            cp.start()
        refs[-1][...] = jnp.zeros_like(refs[-1])

    operands = [pltpu.with_memory_space_constraint(a, pltpu.HBM) for a in list(srcs) + list(lands)]
    outs = pl.pallas_call(
        body, name=name, in_specs=[HBM] * (2 * n) + [ANY],
        out_specs=[SEM, SEM] + [HBM] * n + [pl.BlockSpec(memory_space=pltpu.VMEM)],
        out_shape=[pltpu.SemaphoreType.DMA((n_sem,)), pltpu.SemaphoreType.DMA((n_sem,))]
        + [pltpu.HBM(a.shape, a.dtype) for a in lands] + [jax.ShapeDtypeStruct((8, LANES), F32)],
        input_output_aliases={n + k: 2 + k for k in range(n)},
        compiler_params=pltpu.CompilerParams(has_side_effects=EFFECT),
    )(*operands, after)
    return outs[0], outs[1], outs[2:2 + n], outs[-1]


def _exchange_wait(name, exchange, send_sems, recv_sems, srcs, lands, after):
    copies, _ = exchange
    n = len(srcs)

    def body(*refs):
        for cp in copies(refs[:n], refs[n:2 * n], refs[2 * n], refs[2 * n + 1]):
            cp.wait_send()
            cp.wait_recv()

    return pl.pallas_call(
        body, name=name, in_specs=[HBM] * (2 * n) + [SEM, SEM, ANY], out_specs=[HBM] * n,
        out_shape=[pltpu.HBM(a.shape, a.dtype) for a in lands],
        input_output_aliases={n + k: k for k in range(n)},
        compiler_params=pltpu.CompilerParams(has_side_effects=EFFECT),
    )(*srcs, *lands, send_sems, recv_sems, after)


def _adamw(w, g, m, v):
    m = ADAM_B1 * m + (1.0 - ADAM_B1) * g
    v = ADAM_B2 * v + (1.0 - ADAM_B2) * (g * g)
    m_hat = m / (1.0 - ADAM_B1 ** ADAM_STEP)
    v_hat = v / (1.0 - ADAM_B2 ** ADAM_STEP)
    return -ADAM_LR * (m_hat / (jnp.sqrt(v_hat) + ADAM_EPS) + ADAM_WD * w), m, v


def _reduce_update(name, own, landed, chip, w, m, v, layer, prev):
    _, R, C = landed.shape
    L = w.shape[0]
    tr = _row_tile(R, C, 4, 1 << 20, 16)

    def body(chip_ref, own_ref, p_ref, w_ref, m_ref, v_ref, *rest):
        g_ref, d_ref, mo_ref, vo_ref = rest[-4:]
        mine = chip_ref[0]
        g = None
        for k in range(N_XY):
            term = jnp.where(mine == k, own_ref[...], p_ref[k]).astype(F32)
            g = term if g is None else g + term
        g_ref[...] = g
        d_ref[...], mo_ref[...], vo_ref[...] = _adamw(w_ref[...], g, m_ref[...], v_ref[...])

    lay = pl.BlockSpec((None, tr, C), lambda i, c: (layer, i, 0))
    osh = jax.ShapeDtypeStruct((L, R, C), F32)
    extra = [] if prev is None else list(prev)
    return pl.pallas_call(
        body, name=name,
        grid_spec=pltpu.PrefetchScalarGridSpec(
            num_scalar_prefetch=1, grid=(R // tr,),
            in_specs=[pl.BlockSpec((None, tr, C), lambda i, c: (c[0], i, 0)),
                      pl.BlockSpec((N_XY, tr, C), lambda i, c: (0, i, 0)), lay, lay, lay] + [ANY] * len(extra),
            out_specs=[lay] * 4),
        out_shape=[osh] * 4,
        input_output_aliases={6 + k: k for k in range(len(extra))},
        compiler_params=_params(9 * _nbytes((tr, C), F32)),
    )(chip, own, landed, w, m, v, *extra)


def _all_reduce_small(name, vec):
    R = vec.shape[0]

    def body(v_ref, o_ref, tot_ref, buf, send_sems, recv_sems):
        x, y, c = _position()
        me = 4 * x + 2 * y + c
        buf[me] = v_ref[...]
        sends = []
        for d in range(1, N_DEV):
            cp = pltpu.make_async_remote_copy(
                src_ref=v_ref, dst_ref=buf.at[me], send_sem=send_sems.at[d - 1], recv_sem=recv_sems.at[d - 1],
                device_id=(x ^ (d >> 2), y ^ ((d >> 1) & 1), c ^ (d & 1)), device_id_type=MESH)
            cp.start()
            sends.append(cp)
        for cp in sends:
            cp.wait()
        s = buf[0]
        for k in range(1, N_DEV):
            s = s + buf[k]
        o_ref[...] = s
        tot_ref[...] = jnp.sum(jnp.sum(s[0:8], axis=0, keepdims=True), axis=1, keepdims=True)

    vm = pl.BlockSpec(memory_space=pltpu.VMEM)
    return pl.pallas_call(
        body, name=name, in_specs=[vm], out_specs=[vm, vm],
        out_shape=[jax.ShapeDtypeStruct(vec.shape, F32), jax.ShapeDtypeStruct((1, 1), F32)],
        scratch_shapes=[pltpu.VMEM((N_DEV, R, LANES), F32), pltpu.SemaphoreType.DMA((N_DEV - 1,)),
                        pltpu.SemaphoreType.DMA((N_DEV - 1,))],
    )(vec)


def _adamw_small(name, w, g, m, v):
    def body(w_ref, g_ref, m_ref, v_ref, d_ref, mo_ref, vo_ref):
        d_ref[...], mo_ref[...], vo_ref[...] = _adamw(w_ref[...], g_ref[...], m_ref[...], v_ref[...])

    vm = pl.BlockSpec(memory_space=pltpu.VMEM)
    return pl.pallas_call(
        body, name=name, in_specs=[vm] * 4, out_specs=[vm] * 3,
        out_shape=[jax.ShapeDtypeStruct(w.shape, F32)] * 3,
    )(w, g, m, v)


def _pack(parts, rows):
    flat = jnp.concatenate([p.reshape(-1) for p in parts])
    return jnp.pad(flat, (0, rows * LANES - flat.shape[0])).reshape(rows, LANES)


def _unpack(packed, shapes, skip=0):
    flat, out, off = packed.reshape(-1), [], skip
    for s in shapes:
        n = 1
        for d in s:
            n *= d
        out.append(flat[off:off + n].reshape(s))
        off += n
    return out


def kernel(x, norm_mix_g, norm_ffn_g, sb_w_qkv, sb_g_q, sb_g_k, sb_w_o, pool_w, pool_scale, conv_w_in, conv_w, conv_w_out, ffn_w_gate, ffn_w_up, ffn_w_down, loss_target, m_norm_mix_g, m_norm_ffn_g, m_sb_w_qkv, m_sb_g_q, m_sb_g_k, m_sb_w_o, m_pool_w, m_pool_scale, m_conv_w_in, m_conv_w, m_conv_w_out, m_ffn_w_gate, m_ffn_w_up, m_ffn_w_down, v_norm_mix_g, v_norm_ffn_g, v_sb_w_qkv, v_sb_g_q, v_sb_g_k, v_sb_w_o, v_pool_w, v_pool_scale, v_conv_w_in, v_conv_w, v_conv_w_out, v_ffn_w_gate, v_ffn_w_up, v_ffn_w_down):
    _, T, D = x.shape
    depth = norm_mix_g.shape[0]
    H = D // HEAD_DIM
    G = len(POOL_WINDOWS)
    pool_rows = pool_w.shape[2]
    xs = x[0]

    big = {
        "sb_w_qkv": (sb_w_qkv, m_sb_w_qkv, v_sb_w_qkv), "sb_w_o": (sb_w_o, m_sb_w_o, v_sb_w_o),
        "pool_w": tuple(a.reshape(a.shape[0], G * pool_rows, a.shape[3]) for a in (pool_w, m_pool_w, v_pool_w)),
        "conv_w_in": (conv_w_in, m_conv_w_in, v_conv_w_in), "conv_w_out": (conv_w_out, m_conv_w_out, v_conv_w_out),
        "ffn_w_gate": (ffn_w_gate, m_ffn_w_gate, v_ffn_w_gate), "ffn_w_up": (ffn_w_up, m_ffn_w_up, v_ffn_w_up),
        "ffn_w_down": (ffn_w_down, m_ffn_w_down, v_ffn_w_down),
    }
    w16 = {k: t[0].astype(BF16) for k, t in big.items()}

    me = 4 * lax.axis_index("x") + 2 * lax.axis_index("y") + lax.axis_index("c")

    def layer_tensors(i):
        kind, j = i % 3, i // 3
        names = list([("sb_w_qkv", "sb_w_o"), ("pool_w",), ("conv_w_in", "conv_w_out")][kind])
        tensors = [(w16[nm], j) for nm in names] + [(w16[nm], i) for nm in ("ffn_w_gate", "ffn_w_up", "ffn_w_down")]
        names += ["ffn_w_gate", "ffn_w_up", "ffn_w_down"]
        if kind == 2:
            tensors.append((conv_w, j))
            names.append("conv_w")
        return names, tensors

    def gather_start(i, after):
        names, tensors = layer_tensors(i)
        lands = [lax.dynamic_update_slice(lax.empty((N_DEV,) + a.shape[1:], a.dtype), a[idx][None], (me, 0, 0))
                 for a, idx in tensors]
        exchange, srcs = _gather_copies(tensors), [a for a, _ in tensors]
        return (names, exchange, srcs) + _exchange_start(f"gather_start_l{i}", exchange, srcs, lands, after)

    def gather_wait(i, pending, after):
        names, exchange, srcs, send_sems, recv_sems, lands, _ = pending
        return dict(zip(names, _exchange_wait(f"gather_wait_l{i}", exchange, send_sems, recv_sems, srcs, lands, after)))

    names0, tensors0 = layer_tensors(0)
    gathered = [dict(zip(names0, _all_gather("gather_l0", tensors0)))] + [None] * (depth - 1)

    def std_cols(wb):
        return jnp.transpose(wb, (1, 0, 2)).reshape(wb.shape[1], -1)

    saved = []
    xc = xs
    in_flight = {}
    for i in range(depth):
        kind, j = i % 3, i // 3
        if i in in_flight:
            gathered[i] = gather_wait(i, in_flight.pop(i), xc)
        gw = gathered[i]
        for nxt in ([1, 2] if i == 0 else [i + 2]):
            if nxt < depth:
                in_flight[nxt] = gather_start(nxt, gw["ffn_w_down"])
        follows = tuple(p[-1] for p in in_flight.values())
        s = {"x_in": xc}
        if kind == 0:
            h = _rms_fwd(f"norm_mix_l{i}", xc, norm_mix_g[i:i + 1], BF16, follows)
            qkv = _mm_cols(f"qkv_l{i}", h, gw["sb_w_qkv"], F32)
            o32, o16 = _attn_fwd(f"attn_fwd_l{i}", qkv, sb_g_q[j:j + 1], sb_g_k[j:j + 1], H)
            xc = _mm_res(f"attn_out_l{i}", o16, gw["sb_w_o"].reshape(D, D), xc)
            s.update(h=h, qkv=qkv, o32=o32, o16=o16)
        elif kind == 1:
            hf = _rms_fwd(f"norm_mix_l{i}", xc, norm_mix_g[i:i + 1], F32, follows)
            wp = jnp.transpose(gw["pool_w"].reshape(N_DEV, G, pool_rows, D // G), (1, 0, 2, 3)).reshape(G, D // G, D // G)
            xc, p = _pool_fwd(f"pool_fwd_l{i}", hf, xc, wp, pool_scale[j:j + 1])
            s.update(p=p, wp=wp)
        else:
            h = _rms_fwd(f"norm_mix_l{i}", xc, norm_mix_g[i:i + 1], BF16, follows)
            bcx =_mm_cols(f"conv_in_l{i}", h, gw["conv_w_in"], F32)
            cw_full = jnp.transpose(gw["conv_w"], (1, 0, 2)).reshape(3, D)
            by = _conv_fwd(f"conv_fwd_l{i}", bcx, cw_full)
            xc = _mm_res(f"conv_out_l{i}", by, gw["conv_w_out"].reshape(D, D), xc)
            s.update(h=h, bcx=bcx, by=by, cw_full=cw_full)
        s["x_mid"] = xc
        h2 = _rms_fwd(f"norm_ffn_l{i}", xc, norm_ffn_g[i:i + 1], BF16)
        gate, up, act = _ffn_up(f"ffn_up_l{i}", h2, gw["ffn_w_gate"], gw["ffn_w_up"])
        xc = _ffn_down(f"ffn_down_l{i}", act, gw["ffn_w_down"], xc)
        s.update(h2=h2, gate=gate, up=up, act=act)
        saved.append(s)

    dx, dx16, loss_part = _loss_head("loss_head", xc, loss_target[0])

    outs = {k: None for k in big}
    core = lax.axis_index("c").astype(jnp.int32).reshape(1)
    chip = (2 * lax.axis_index("x") + lax.axis_index("y")).astype(jnp.int32).reshape(1)

    def reduce_start(tag, grads):
        names = list(grads)
        got = _swap_halves(f"swap_core_{tag}", [grads[nm][0] for nm in names])
        partial = [_chip_partial(f"chip_sum_{nm}_{tag}", grads[nm][0], b, core) for nm, b in zip(names, got)]
        exchange = _chip_copies(len(names))
        lands = [lax.empty(p.shape, p.dtype) for p in partial]
        started = _exchange_start(f"swap_chip_start_{tag}", exchange, partial, lands, got[0])
        return (tag, [(nm, grads[nm][1]) for nm in names], exchange, partial) + started

    def reduce_finish(pending, after):
        tag, layers, exchange, srcs, send_sems, recv_sems, lands, _ = pending
        lands = _exchange_wait(f"swap_chip_wait_{tag}", exchange, send_sems, recv_sems, srcs, lands, after)
        for (nm, layer), own, landed in zip(layers, srcs, lands):
            w, m, v = big[nm]
            outs[nm] = _reduce_update(f"update_{nm}_{tag}", own, landed, chip, w, m, v, layer, outs[nm])

    d_mix, d_ffn = [None] * depth, [None] * depth
    small = {}
    mix_reducing = None
    for i in reversed(range(depth)):
        kind, j = i % 3, i // 3
        gw, s = gathered[i], saved[i]
        grads = {}
        dgate, dup = _ffn_dact(f"ffn_dact_l{i}", dx16, gw["ffn_w_down"], s["gate"], s["up"],
                               () if mix_reducing is None else (mix_reducing[-1],))
        dwd = _wgrad_rows(f"ffn_dwd_l{i}", s["act"], dx16)
        dh2 = _ffn_dh(f"ffn_dh_l{i}", dgate, dup, gw["ffn_w_gate"], gw["ffn_w_up"])
        dwg, dwu = _wgrad_cols(f"ffn_dwgu_l{i}", s["h2"], [dgate, dup])
        if mix_reducing is not None:
            reduce_finish(mix_reducing, dwu)
        ffn_reducing = reduce_start(f"ffn_l{i}", {"ffn_w_down": (dwd, i), "ffn_w_gate": (dwg, i), "ffn_w_up": (dwu, i)})
        dx, dx16, d_ffn[i] = _rms_bwd(f"norm_ffn_bwd_l{i}", dh2, s["x_mid"], norm_ffn_g[i:i + 1], dx,
                                      (ffn_reducing[-1],))
        if kind == 0:
            wo = gw["sb_w_o"]
            do16 = _mm_nt_rows(f"attn_do_l{i}", dx16, wo, BF16)
            grads["sb_w_o"] = (_wgrad_std(f"attn_dwo_l{i}", s["o16"], dx16).reshape(wo.shape), j)
            d3, dgq, dgk = _attn_bwd(f"attn_bwd_l{i}", s["qkv"], sb_g_q[j:j + 1], sb_g_k[j:j + 1], do16, s["o32"], H)
            small[("sb_g_q", j)], small[("sb_g_k", j)] = dgq, dgk
            dh = _mm_nt_sections(f"qkv_dh_l{i}", d3, std_cols(gw["sb_w_qkv"]))
            grads["sb_w_qkv"] = (_wgrad_sections(f"qkv_dw_l{i}", s["h"], d3, gw["sb_w_qkv"].shape[2]), j)
        elif kind == 1:
            dh, dwp, dps = _pool_bwd(f"pool_bwd_l{i}", dx, s["p"], s["wp"], pool_scale[j:j + 1])
            small[("pool_scale", j)] = dps
            dwp = jnp.transpose(dwp.reshape(G, N_DEV, pool_rows, D // G), (1, 0, 2, 3)).reshape(N_DEV, G * pool_rows, D // G)
            grads["pool_w"] = (dwp, j)
        else:
            wout = gw["conv_w_out"]
            dby = _mm_nt_rows(f"conv_dby_l{i}", dx16, wout, F32)
            grads["conv_w_out"] = (_wgrad_std(f"conv_dwout_l{i}", s["by"], dx16).reshape(wout.shape), j)
            d3, dcw = _conv_bwd(f"conv_bwd_l{i}", dby, s["bcx"], s["cw_full"])
            small[("conv_w", j)] = dcw[0:3]
            dh = _mm_nt_sections(f"conv_dh_l{i}", d3, std_cols(gw["conv_w_in"]))
            grads["conv_w_in"] = (_wgrad_sections(f"conv_dwin_l{i}", s["h"], d3, gw["conv_w_in"].shape[2]), j)
        dx, dx16, d_mix[i] = _rms_bwd(f"norm_mix_bwd_l{i}", dh, s["x_in"], norm_mix_g[i:i + 1], dx)
        reduce_finish(ffn_reducing, d_mix[i])
        mix_reducing = reduce_start(f"mix_l{i}", grads)

    n_sb, n_pool, n_conv = sb_g_q.shape[0], pool_scale.shape[0], conv_w.shape[0]
    pieces = [loss_part, jnp.concatenate(d_mix), jnp.concatenate(d_ffn),
              jnp.concatenate([small[("sb_g_q", j)] for j in range(n_sb)]),
              jnp.concatenate([small[("sb_g_k", j)] for j in range(n_sb)]),
              jnp.concatenate([small[("pool_scale", j)] for j in range(n_pool)]),
              jnp.stack([small[("conv_w", j)] for j in range(n_conv)])]
    n_small = sum(p.size for p in pieces)
    rows = -(-n_small // (8 * LANES)) * 8
    summed, loss = _all_reduce_small("reduce_small", _pack(pieces, rows))
    shapes = [norm_mix_g.shape, norm_ffn_g.shape, sb_g_q.shape, sb_g_k.shape, pool_scale.shape, (n_conv, 3, D)]
    g_mix, g_ffn, g_q, g_k, g_ps, g_cw_full = _unpack(summed, shapes, skip=8 * LANES)
    cshard = conv_w.shape[2]
    me = 4 * lax.axis_index("x") + 2 * lax.axis_index("y") + lax.axis_index("c")
    g_cw = lax.dynamic_slice_in_dim(g_cw_full, me * cshard, cshard, axis=2)
    small_names = ["norm_mix_g", "norm_ffn_g", "sb_g_q", "sb_g_k", "pool_scale", "conv_w"]
    small_w = [norm_mix_g, norm_ffn_g, sb_g_q, sb_g_k, pool_scale, conv_w]
    small_m = [m_norm_mix_g, m_norm_ffn_g, m_sb_g_q, m_sb_g_k, m_pool_scale, m_conv_w]
    small_v = [v_norm_mix_g, v_norm_ffn_g, v_sb_g_q, v_sb_g_k, v_pool_scale, v_conv_w]
    small_g = [g_mix, g_ffn, g_q, g_k, g_ps, g_cw]
    n_upd = sum(a.size for a in small_w)
    urows = -(-n_upd // (8 * LANES)) * 8
    sd, sm, sv = _adamw_small("update_small", _pack(small_w, urows), _pack(small_g, urows),
                              _pack(small_m, urows), _pack(small_v, urows))
    reduce_finish(mix_reducing, sd)
    sshapes = [a.shape for a in small_w]
    res = {nm: (g, d, m_, v_) for nm, g, d, m_, v_ in
           zip(small_names, small_g, _unpack(sd, sshapes), _unpack(sm, sshapes), _unpack(sv, sshapes))}
    for nm, stacks in outs.items():
        shape = pool_w.shape if nm == "pool_w" else big[nm][0].shape
        res[nm] = tuple(a.reshape(shape) for a in stacks)

    order = ["norm_mix_g", "norm_ffn_g", "sb_w_qkv", "sb_g_q", "sb_g_k", "sb_w_o", "pool_w", "pool_scale",
             "conv_w_in", "conv_w", "conv_w_out", "ffn_w_gate", "ffn_w_up", "ffn_w_down"]
    return (loss.reshape(()), dx[None], *[res[nm][0] for nm in order], *[res[nm][1] for nm in order],
            *[res[nm][2] for nm in order], *[res[nm][3] for nm in order])
```

```python
import functools

import jax
import jax.numpy as jnp
from jax import lax
from jax.experimental import pallas as pl
from jax.experimental.pallas import tpu as pltpu

F32 = jnp.float32
BF16 = jnp.bfloat16

N_DEV = 8
N_XY = 4
HEAD_DIM = 128
LANES = 128
POOL_WINDOWS = (2, 4, 8, 16)
POOL_HALO = 16
CONV_HALO = 8
ATTN_BLOCK = 512
SHARDS_PER_STEP = 2
WGRAD_TOKENS = 2048
WGRAD_TOKENS_NARROW = 4096
EPS = 1e-6
ADAM_LR = 0.001
ADAM_B1 = 0.9
ADAM_B2 = 0.999
ADAM_EPS = 1e-08
ADAM_WD = 0.01
ADAM_STEP = 10

VMEM_CAP_V7X = 56 * 1024 * 1024
VMEM_FLOOR = 32 * 1024 * 1024

NN = (((1,), (0,)), ((), ()))
NT = (((1,), (1,)), ((), ()))
TN = (((0,), (0,)), ((), ()))
MESH = pl.DeviceIdType.MESH
ANY = pl.BlockSpec(memory_space=pl.ANY)


def _dot(a, b, dims):
    return lax.dot_general(a, b, dims, preferred_element_type=F32)


def _params(step_bytes, scratch_bytes=0):
    need = 2 * step_bytes + scratch_bytes + 3 * step_bytes // 2 + (4 << 20)
    return pltpu.CompilerParams(vmem_limit_bytes=int(min(VMEM_CAP_V7X, max(VMEM_FLOOR, need))))


def _nbytes(shape, dtype):
    n = 1
    for s in shape:
        if s is not None:
            n *= s
    return n * jnp.dtype(dtype).itemsize


def _tile(n, target, mult=8):
    t = min(n, target)
    t -= t % mult
    while t > mult and n % t:
        t -= mult
    assert t >= mult and n % t == 0, (n, target, mult)
    return t


def _row_tile(rows, cols, itemsize, budget, mult=16):
    return _tile(rows, max(mult, budget // (cols * itemsize)), mult)


def _rms_fwd(name, x, g_row, out_dtype, follows=()):
    T, D = x.shape
    tm = _tile(T, 512)

    def body(x_ref, g_ref, *rest):
        o_ref = rest[-1]
        xv = x_ref[...]
        r = lax.rsqrt(jnp.mean(xv * xv, axis=-1, keepdims=True) + EPS)
        o_ref[...] = (xv * r * g_ref[...]).astype(o_ref.dtype)

    return pl.pallas_call(
        body, name=name, grid=(T // tm,),
        in_specs=[pl.BlockSpec((tm, D), lambda i: (i, 0)), pl.BlockSpec((1, D), lambda i: (0, 0))] + [ANY] * len(follows),
        out_specs=pl.BlockSpec((tm, D), lambda i: (i, 0)),
        out_shape=jax.ShapeDtypeStruct((T, D), out_dtype),
        compiler_params=_params(_nbytes((tm, D), F32) * 2),
    )(x, g_row, *follows)


def _rms_bwd(name, dh, x, g_row, dres, follows=()):
    T, D = x.shape
    tm = _tile(T, 256)

    def body(dh_ref, x_ref, g_ref, dres_ref, *rest):
        dx_ref, dx16_ref, dg_ref = rest[-3:]
        xv = x_ref[...]
        r = lax.rsqrt(jnp.mean(xv * xv, axis=-1, keepdims=True) + EPS)
        xh = xv * r
        dhv = dh_ref[...]
        dy = dhv * g_ref[...]
        m = jnp.mean(dy * xh, axis=-1, keepdims=True)
        dx = dres_ref[...] + r * (dy - xh * m)
        dx_ref[...] = dx
        dx16_ref[...] = dx.astype(BF16)

        @pl.when(pl.program_id(0) == 0)
        def _():
            dg_ref[...] = jnp.zeros_like(dg_ref)

        dg_ref[...] += jnp.sum(dhv * xh, axis=0, keepdims=True)

    blk = pl.BlockSpec((tm, D), lambda i: (i, 0))
    row = pl.BlockSpec((1, D), lambda i: (0, 0))
    return pl.pallas_call(
        body, name=name, grid=(T // tm,),
        in_specs=[blk, blk, row, blk] + [ANY] * len(follows), out_specs=[blk, blk, row],
        out_shape=[jax.ShapeDtypeStruct((T, D), F32), jax.ShapeDtypeStruct((T, D), BF16),
                   jax.ShapeDtypeStruct((1, D), F32)],
        compiler_params=_params(_nbytes((tm, D), F32) * 5),
    )(dh, x, g_row, dres, *follows)


def _loss_head(name, y, target):
    T, D = y.shape
    tm = _tile(T, 256)

    def body(y_ref, t_ref, dy_ref, dy16_ref, acc_ref):
        e = y_ref[...] - t_ref[...]
        d = e * (1.0 / D)
        dy_ref[...] = d
        dy16_ref[...] = d.astype(BF16)
        s = (e * e).reshape(tm // 8, 8, D).sum(axis=0)
        part = s[:, 0:LANES]
        for k in range(1, D // LANES):
            part = part + s[:, k * LANES:(k + 1) * LANES]

        @pl.when(pl.program_id(0) == 0)
        def _():
            acc_ref[...] = jnp.zeros_like(acc_ref)

        acc_ref[...] += part * (0.5 / D)

    blk = pl.BlockSpec((tm, D), lambda i: (i, 0))
    return pl.pallas_call(
        body, name=name, grid=(T // tm,),
        in_specs=[blk, blk], out_specs=[blk, blk, pl.BlockSpec((8, LANES), lambda i: (0, 0))],
        out_shape=[jax.ShapeDtypeStruct((T, D), F32), jax.ShapeDtypeStruct((T, D), BF16),
                   jax.ShapeDtypeStruct((8, LANES), F32)],
        compiler_params=_params(_nbytes((tm, D), F32) * 4),
    )(y, target)


def _mm_cols(name, a, w, out_dtype):
    T, K = a.shape
    nb, _, ns = w.shape
    tm = _tile(T, 1024)

    def body(a_ref, w_ref, o_ref):
        o_ref[...] = _dot(a_ref[...], w_ref[...], NN).astype(o_ref.dtype)

    step = _nbytes((tm, K), BF16) + _nbytes((K, ns), BF16) + _nbytes((tm, ns), out_dtype)
    return pl.pallas_call(
        body, name=name, grid=(nb, T // tm),
        in_specs=[pl.BlockSpec((tm, K), lambda b, i: (i, 0)),
                  pl.BlockSpec((None, K, ns), lambda b, i: (b, 0, 0))],
        out_specs=pl.BlockSpec((None, tm, ns), lambda b, i: (b, i, 0)),
        out_shape=jax.ShapeDtypeStruct((nb, T, ns), out_dtype),
        compiler_params=_params(step),
    )(a, w)


def _ffn_up(name, h, wg, wu):
    T, K = h.shape
    nb, _, fs = wg.shape
    tm = _tile(T, 1024)

    def body(a_ref, wg_ref, wu_ref, g_ref, u_ref, act_ref):
        av = a_ref[...]
        g = _dot(av, wg_ref[...], NN)
        u = _dot(av, wu_ref[...], NN)
        g_ref[...] = g
        u_ref[...] = u
        act_ref[...] = (g * jax.nn.sigmoid(g) * u).astype(BF16)

    wspec = pl.BlockSpec((None, K, fs), lambda b, i: (b, 0, 0))
    ospec = pl.BlockSpec((None, tm, fs), lambda b, i: (b, i, 0))
    osh = lambda dt: jax.ShapeDtypeStruct((nb, T, fs), dt)
    step = _nbytes((tm, K), BF16) + 2 * _nbytes((K, fs), BF16) + _nbytes((tm, fs), BF16) + 2 * _nbytes((tm, fs), F32)
    return pl.pallas_call(
        body, name=name, grid=(nb, T // tm),
        in_specs=[pl.BlockSpec((tm, K), lambda b, i: (i, 0)), wspec, wspec],
        out_specs=[ospec, ospec, ospec], out_shape=[osh(F32), osh(F32), osh(BF16)],
        compiler_params=_params(step),
    )(h, wg, wu)


def _mm_res(name, a, w, res):
    T, K = a.shape
    N = w.shape[1]
    tm, tn = _tile(T, 512), _tile(N, 1024, LANES)

    def body(a_ref, w_ref, r_ref, o_ref):
        o_ref[...] = r_ref[...] + _dot(a_ref[...], w_ref[...], NN)

    step = _nbytes((tm, K), BF16) + _nbytes((K, tn), BF16) + 2 * _nbytes((tm, tn), F32)
    return pl.pallas_call(
        body, name=name, grid=(N // tn, T // tm),
        in_specs=[pl.BlockSpec((tm, K), lambda j, i: (i, 0)), pl.BlockSpec((K, tn), lambda j, i: (0, j)),
                  pl.BlockSpec((tm, tn), lambda j, i: (i, j))],
        out_specs=pl.BlockSpec((tm, tn), lambda j, i: (i, j)),
        out_shape=jax.ShapeDtypeStruct((T, N), F32),
        compiler_params=_params(step),
    )(a, w, res)


def _ffn_down(name, act, wd, res):
    nb, T, fs = act.shape
    N = wd.shape[2]
    tm, tn = _tile(T, 1024), _tile(N, 1024, LANES)
    sp = SHARDS_PER_STEP
    steps = nb // sp

    def body(a_ref, w_ref, r_ref, o_ref, acc_ref):
        b = pl.program_id(2)

        @pl.when(b == 0)
        def _():
            acc_ref[...] = r_ref[...]

        acc_ref[...] += sum(_dot(a_ref[u], w_ref[u], NN) for u in range(sp))

        @pl.when(b == steps - 1)
        def _():
            o_ref[...] = acc_ref[...]

    step = sp * (_nbytes((tm, fs), BF16) + _nbytes((fs, tn), BF16)) + 2 * _nbytes((tm, tn), F32)
    return pl.pallas_call(
        body, name=name, grid=(T // tm, N // tn, steps),
        in_specs=[pl.BlockSpec((sp, tm, fs), lambda i, j, b: (b, i, 0)),
                  pl.BlockSpec((sp, fs, tn), lambda i, j, b: (b, 0, j)),
                  pl.BlockSpec((tm, tn), lambda i, j, b: (i, j))],
        out_specs=pl.BlockSpec((tm, tn), lambda i, j, b: (i, j)),
        out_shape=jax.ShapeDtypeStruct((T, N), F32),
        scratch_shapes=[pltpu.VMEM((tm, tn), F32)],
        compiler_params=_params(step, _nbytes((tm, tn), F32)),
    )(act, wd, res)


def _mm_nt_rows(name, g, w, out_dtype):
    T, N = g.shape
    nb, ks, _ = w.shape
    tm = _tile(T, 1024)

    def body(g_ref, w_ref, o_ref):
        o_ref[...] = _dot(g_ref[...], w_ref[...], NT).astype(o_ref.dtype)

    step = _nbytes((tm, N), BF16) + _nbytes((ks, N), BF16) + _nbytes((tm, ks), out_dtype)
    return pl.pallas_call(
        body, name=name, grid=(nb, T // tm),
        in_specs=[pl.BlockSpec((tm, N), lambda b, i: (i, 0)),
                  pl.BlockSpec((None, ks, N), lambda b, i: (b, 0, 0))],
        out_specs=pl.BlockSpec((tm, ks), lambda b, i: (i, b)),
        out_shape=jax.ShapeDtypeStruct((T, nb * ks), out_dtype),
        compiler_params=_params(step),
    )(g, w)


def _ffn_dact(name, dx16, wd, gate, up, follows=()):
    T, N = dx16.shape
    nb, fs, _ = wd.shape
    tm = _tile(T, 1024)

    def body(g_ref, w_ref, gate_ref, up_ref, *rest):
        dg_ref, du_ref = rest[-2:]
        dact = _dot(g_ref[...], w_ref[...], NT)
        gt = gate_ref[...]
        s = jax.nn.sigmoid(gt)
        silu = gt * s
        dg_ref[...] = (dact * up_ref[...] * (s * (1.0 + gt * (1.0 - s)))).astype(BF16)
        du_ref[...] = (dact * silu).astype(BF16)

    aspec = pl.BlockSpec((None, tm, fs), lambda b, i: (b, i, 0))
    osh = jax.ShapeDtypeStruct((nb, T, fs), BF16)
    step = _nbytes((tm, N), BF16) + _nbytes((fs, N), BF16) + 2 * _nbytes((tm, fs), BF16) + 3 * _nbytes((tm, fs), F32)
    return pl.pallas_call(
        body, name=name, grid=(nb, T // tm),
        in_specs=[pl.BlockSpec((tm, N), lambda b, i: (i, 0)),
                  pl.BlockSpec((None, fs, N), lambda b, i: (b, 0, 0)), aspec, aspec] + [ANY] * len(follows),
        out_specs=[aspec, aspec], out_shape=[osh, osh],
        compiler_params=_params(step),
    )(dx16, wd, gate, up, *follows)


def _ffn_dh(name, dgate, dup, wg, wu, follows=()):
    nb, T, fs = dgate.shape
    D = wg.shape[1]
    tm, tn = _tile(T, 1024), _tile(D, 1024, LANES)
    sp = SHARDS_PER_STEP
    steps = nb // sp

    def body(dg_ref, du_ref, wg_ref, wu_ref, *rest):
        o_ref, acc_ref = rest[-2:]
        b = pl.program_id(2)

        @pl.when(b == 0)
        def _():
            acc_ref[...] = jnp.zeros_like(acc_ref)

        acc_ref[...] += sum(_dot(dg_ref[u], wg_ref[u], NT) + _dot(du_ref[u], wu_ref[u], NT) for u in range(sp))

        @pl.when(b == steps - 1)
        def _():
            o_ref[...] = acc_ref[...]

    aspec = pl.BlockSpec((sp, tm, fs), lambda i, j, b: (b, i, 0))
    wspec = pl.BlockSpec((sp, tn, fs), lambda i, j, b: (b, j, 0))
    step = sp * (2 * _nbytes((tm, fs), BF16) + 2 * _nbytes((tn, fs), BF16)) + _nbytes((tm, tn), F32)
    return pl.pallas_call(
        body, name=name, grid=(T // tm, D // tn, steps),
        in_specs=[aspec, aspec, wspec, wspec] + [ANY] * len(follows),
        out_specs=pl.BlockSpec((tm, tn), lambda i, j, b: (i, j)),
        out_shape=jax.ShapeDtypeStruct((T, D), F32),
        scratch_shapes=[pltpu.VMEM((tm, tn), F32)],
        compiler_params=_params(step, _nbytes((tm, tn), F32)),
    )(dgate, dup, wg, wu, *follows)


def _mm_nt_sections(name, g3, w_std):
    ns_, T, Ds = g3.shape
    D = w_std.shape[0]
    tm, tn, tk = _tile(T, 1024), _tile(D, 1024, LANES), _tile(Ds, 2048, LANES)
    nkk = Ds // tk
    nk = ns_ * nkk

    def body(g_ref, w_ref, o_ref, acc_ref):
        r = pl.program_id(2)

        @pl.when(r == 0)
        def _():
            acc_ref[...] = jnp.zeros_like(acc_ref)

        acc_ref[...] += _dot(g_ref[...], w_ref[...], NT)

        @pl.when(r == nk - 1)
        def _():
            o_ref[...] = acc_ref[...]

    step = _nbytes((tm, tk), BF16) + _nbytes((tn, tk), BF16) + _nbytes((tm, tn), F32)
    return pl.pallas_call(
        body, name=name, grid=(T // tm, D // tn, nk),
        in_specs=[pl.BlockSpec((None, tm, tk), lambda i, j, r: (r // nkk, i, r % nkk)),
                  pl.BlockSpec((tn, tk), lambda i, j, r: (j, r))],
        out_specs=pl.BlockSpec((tm, tn), lambda i, j, r: (i, j)),
        out_shape=jax.ShapeDtypeStruct((T, D), F32),
        scratch_shapes=[pltpu.VMEM((tm, tn), F32)],
        compiler_params=_params(step, _nbytes((tm, tn), F32)),
    )(g3, w_std)


def _wgrad(name, grid, a, a_spec, gs, g_spec, out_shape, o_spec, acc_shape):
    n = len(gs)
    nt = grid[-1]

    def body(*refs):
        a_ref, g_refs, o_refs, acc_refs = refs[0], refs[1:1 + n], refs[1 + n:1 + 2 * n], refs[1 + 2 * n:]
        t = pl.program_id(len(grid) - 1)
        av = a_ref[...]
        for g_ref, o_ref, acc_ref in zip(g_refs, o_refs, acc_refs):
            @pl.when(t == 0)
            def _():
                acc_ref[...] = jnp.zeros_like(acc_ref)

            acc_ref[...] += _dot(av, g_ref[...], TN)

            @pl.when(t == nt - 1)
            def _():
                o_ref[...] = acc_ref[...]

    step = _nbytes(a_spec.block_shape, BF16) + n * (_nbytes(g_spec.block_shape, BF16) + _nbytes(acc_shape, F32))
    outs = pl.pallas_call(
        body, name=name, grid=grid,
        in_specs=[a_spec] + [g_spec] * n, out_specs=[o_spec] * n,
        out_shape=[jax.ShapeDtypeStruct(out_shape, F32)] * n,
        scratch_shapes=[pltpu.VMEM(acc_shape, F32)] * n,
        compiler_params=_params(step, n * _nbytes(acc_shape, F32)),
    )(a, *gs)
    return outs


def _wgrad_cols(name, a, gs):
    T, K = a.shape
    nb, _, ns = gs[0].shape
    tk, tt = _tile(K, 1024, LANES), _tile(T, WGRAD_TOKENS)
    return _wgrad(name, (nb, K // tk, T // tt), a,
                  pl.BlockSpec((tt, tk), lambda b, k, t: (t, k)), gs,
                  pl.BlockSpec((None, tt, ns), lambda b, k, t: (b, t, 0)),
                  (nb, K, ns), pl.BlockSpec((None, tk, ns), lambda b, k, t: (b, k, 0)), (tk, ns))


def _wgrad_rows(name, act, dx16):
    nb, T, fs = act.shape
    N = dx16.shape[1]
    tn, tt = _tile(N, 1024, LANES), _tile(T, WGRAD_TOKENS)
    return _wgrad(name, (nb, N // tn, T // tt), act,
                  pl.BlockSpec((None, tt, fs), lambda b, j, t: (b, t, 0)), [dx16],
                  pl.BlockSpec((tt, tn), lambda b, j, t: (t, j)),
                  (nb, fs, N), pl.BlockSpec((None, fs, tn), lambda b, j, t: (b, 0, j)), (fs, tn))[0]


def _wgrad_std(name, a, g):
    T, K = a.shape
    N = g.shape[1]
    tk, tn, tt = _tile(K, 1024, LANES), _tile(N, 1024, LANES), _tile(T, WGRAD_TOKENS)
    return _wgrad(name, (K // tk, N // tn, T // tt), a,
                  pl.BlockSpec((tt, tk), lambda k, j, t: (t, k)), [g],
                  pl.BlockSpec((tt, tn), lambda k, j, t: (t, j)),
                  (K, N), pl.BlockSpec((tk, tn), lambda k, j, t: (k, j)), (tk, tn))[0]


def _wgrad_sections(name, a, g3, ns):
    T, K = a.shape
    nsec, _, Ds = g3.shape
    cw = 256 if (ns % 256 == 0 and Ds % 256 == 0) else LANES
    per_sec, per_shard = Ds // cw, ns // cw
    nb = nsec * Ds // ns
    tk, tt = _tile(K, 1024, LANES), _tile(T, WGRAD_TOKENS_NARROW)
    return _wgrad(name, (K // tk, nsec * per_sec, T // tt), a,
                  pl.BlockSpec((tt, tk), lambda k, p, t: (t, k)), [g3],
                  pl.BlockSpec((None, tt, cw), lambda k, p, t: (p // per_sec, t, p % per_sec)),
                  (nb, K, ns), pl.BlockSpec((None, tk, cw), lambda k, p, t: (p // per_shard, k, p % per_shard)),
                  (tk, cw))[0]


def _tri2():
    r = lax.broadcasted_iota(jnp.int32, (LANES, 2 * LANES), 0)
    c = lax.broadcasted_iota(jnp.int32, (LANES, 2 * LANES), 1)
    return jnp.where((r >= c) | (c >= LANES), 1.0, 0.0).astype(BF16)


def _suffix_sums(v, tri):
    hi = v.astype(BF16)
    lo = (v - hi.astype(F32)).astype(BF16)
    both = _dot(hi, tri, NN) + _dot(lo, tri, NN)
    return both[:, :LANES], both[:, LANES:]


def _suffix_sums_wide(v, carry, tri):
    pieces = [None] * (v.shape[1] // LANES)
    for u in reversed(range(len(pieces))):
        incl, tot = _suffix_sums(v[:, u * LANES:(u + 1) * LANES], tri)
        pieces[u] = incl + carry
        carry = carry + tot
    return jnp.concatenate(pieces, axis=1), carry


def _sb_group(z, mask, carry, tri):
    lsn = -(jnp.maximum(z, 0.0) + jnp.log(1.0 + jnp.exp(-jnp.abs(z))))
    lk = lsn if mask is None else jnp.where(mask, lsn, 0.0)
    incl, carry = _suffix_sums_wide(lk, carry, tri)
    a = jnp.exp(z + lsn + (incl - lk))
    return (a if mask is None else jnp.where(mask, a, 0.0)), lsn, carry


def _head_norm_store(src_ref, g_ref, dst_ref, rows, chunk):
    def step(i, _):
        r0 = pl.multiple_of(i * chunk, chunk)
        v = src_ref[pl.ds(r0, chunk), :]
        r = lax.rsqrt(jnp.mean(v * v, axis=-1, keepdims=True) + EPS)
        dst_ref[pl.ds(r0, chunk), :] = (v * r * g_ref[...]).astype(dst_ref.dtype)
        return 0

    lax.fori_loop(0, rows // chunk, step, 0)


def _qkv_specs(T, ns, H):
    cps = ns // HEAD_DIM

    def spec(sec):
        return pl.BlockSpec((None, T, HEAD_DIM), lambda h: ((sec * H + h) // cps, 0, (sec * H + h) % cps))

    return [spec(0), spec(1), spec(2)]


def _attn_fwd(name, qkv, gq, gk, H):
    _, T, ns = qkv.shape
    D = H * HEAD_DIM
    BQ = _tile(T, ATTN_BLOCK, LANES)
    scale = HEAD_DIM ** -0.5

    def body(q_ref, k_ref, v_ref, gq_ref, gk_ref, o32_ref, o16_ref, qn, kn, vb):
        _head_norm_store(q_ref, gq_ref, qn, T, BQ)
        _head_norm_store(k_ref, gk_ref, kn, T, BQ)
        vb[...] = v_ref[...].astype(BF16)
        tri = _tri2()
        causal = lax.broadcasted_iota(jnp.int32, (BQ, BQ), 1) < lax.broadcasted_iota(jnp.int32, (BQ, BQ), 0)

        def qloop(qi, _):
            t0 = pl.multiple_of(qi * BQ, BQ)
            qb = qn[pl.ds(t0, BQ), :]

            def step(grp, carry, mask):
                o_acc, cr = carry
                s0 = pl.multiple_of(grp * BQ, BQ)
                z = _dot(qb, kn[pl.ds(s0, BQ), :], NT) * scale
                a, _, cr = _sb_group(z, mask, cr, tri)
                o_acc = o_acc + _dot(a.astype(BF16), vb[pl.ds(s0, BQ), :], NN)
                return o_acc, cr

            zero = jnp.zeros((BQ, LANES), F32)
            first = step(qi, (zero, zero), causal)
            o_acc, _ = lax.fori_loop(0, qi, lambda jj, c: step(qi - 1 - jj, c, None), first)
            o32_ref[pl.ds(t0, BQ), :] = o_acc
            o16_ref[pl.ds(t0, BQ), :] = o_acc.astype(BF16)
            return 0

        lax.fori_loop(0, T // BQ, qloop, 0)

    gspec = pl.BlockSpec((1, HEAD_DIM), lambda h: (0, 0))
    ospec = pl.BlockSpec((T, HEAD_DIM), lambda h: (0, h))
    step = 3 * _nbytes((T, HEAD_DIM), F32) + _nbytes((T, HEAD_DIM), F32) + _nbytes((T, HEAD_DIM), BF16)
    return pl.pallas_call(
        body, name=name, grid=(H,),
        in_specs=_qkv_specs(T, ns, H) + [gspec, gspec],
        out_specs=[ospec, ospec],
        out_shape=[jax.ShapeDtypeStruct((T, D), F32), jax.ShapeDtypeStruct((T, D), BF16)],
        scratch_shapes=[pltpu.VMEM((T, HEAD_DIM), BF16)] * 3,
        compiler_params=_params(step, 3 * _nbytes((T, HEAD_DIM), BF16) + 10 * _nbytes((BQ, BQ), F32)),
    )(qkv, qkv, qkv, gq, gk)


def _attn_bwd(name, qkv, gq, gk, do16, o32, H):
    _, T, ns = qkv.shape
    D = H * HEAD_DIM
    BQ = _tile(T, ATTN_BLOCK, LANES)
    scale = HEAD_DIM ** -0.5

    def body(q_ref, k_ref, v_ref, gq_ref, gk_ref, do_ref, o_ref, d3_ref, dgq_ref, dgk_ref,
             qn, kn, vb, dqn, dkn, dv):
        h = pl.program_id(0)
        _head_norm_store(q_ref, gq_ref, qn, T, BQ)
        _head_norm_store(k_ref, gk_ref, kn, T, BQ)
        vb[...] = v_ref[...].astype(BF16)
        dkn[...] = jnp.zeros_like(dkn)
        dv[...] = jnp.zeros_like(dv)
        tri = _tri2()
        causal = lax.broadcasted_iota(jnp.int32, (BQ, BQ), 1) < lax.broadcasted_iota(jnp.int32, (BQ, BQ), 0)

        def qloop(qi, _):
            t0 = pl.multiple_of(qi * BQ, BQ)
            qb = qn[pl.ds(t0, BQ), :]
            dob = do_ref[pl.ds(t0, BQ), :]
            total = jnp.sum(dob.astype(F32) * o_ref[pl.ds(t0, BQ), :], axis=-1, keepdims=True)

            def step(grp, carry, mask):
                dq_acc, cr, crd = carry
                s0 = pl.multiple_of(grp * BQ, BQ)
                kb = kn[pl.ds(s0, BQ), :]
                vj = vb[pl.ds(s0, BQ), :]
                z = _dot(qb, kb, NT) * scale
                a, lsn, cr = _sb_group(z, mask, cr, tri)
                ab = a.astype(BF16)
                dla = ab.astype(F32) * _dot(dob, vj, NT)
                later, crd = _suffix_sums_wide(dla, crd, tri)
                sig = jnp.exp(z + lsn)
                d_keep = (total - later) * sig
                dz = (dla * (1.0 - sig) - (d_keep if mask is None else jnp.where(mask, d_keep, 0.0))) * scale
                dzb = dz.astype(BF16)
                dq_acc = dq_acc + _dot(dzb, kb, NN)
                dkn[pl.ds(s0, BQ), :] += _dot(dzb, qb, TN)
                dv[pl.ds(s0, BQ), :] += _dot(ab, dob, TN)
                return dq_acc, cr, crd

            zero = jnp.zeros((BQ, LANES), F32)
            first = step(qi, (zero, zero, zero), causal)
            dq_acc, _, _ = lax.fori_loop(0, qi, lambda jj, c: step(qi - 1 - jj, c, None), first)
            dqn[pl.ds(t0, BQ), :] = dq_acc
            return 0

        lax.fori_loop(0, T // BQ, qloop, 0)

        @pl.when(h == 0)
        def _():
            dgq_ref[...] = jnp.zeros_like(dgq_ref)
            dgk_ref[...] = jnp.zeros_like(dgk_ref)

        def norm_bwd(src_ref, g_ref, dy_ref, sec, dg_ref):
            def step(i, _):
                r0 = pl.multiple_of(i * BQ, BQ)
                v = src_ref[pl.ds(r0, BQ), :]
                r = lax.rsqrt(jnp.mean(v * v, axis=-1, keepdims=True) + EPS)
                vh = v * r
                dyo = dy_ref[pl.ds(r0, BQ), :]
                dy = dyo * g_ref[...]
                m = jnp.mean(dy * vh, axis=-1, keepdims=True)
                d3_ref[sec, pl.ds(r0, BQ), :] = (r * (dy - vh * m)).astype(BF16)
                dg_ref[...] += jnp.sum(dyo * vh, axis=0, keepdims=True)
                return 0

            lax.fori_loop(0, T // BQ, step, 0)

        norm_bwd(q_ref, gq_ref, dqn, 0, dgq_ref)
        norm_bwd(k_ref, gk_ref, dkn, 1, dgk_ref)
        d3_ref[2] = dv[...].astype(BF16)

    gspec = pl.BlockSpec((1, HEAD_DIM), lambda h: (0, 0))
    hspec = pl.BlockSpec((T, HEAD_DIM), lambda h: (0, h))
    step = (3 * _nbytes((T, HEAD_DIM), F32) + _nbytes((T, HEAD_DIM), BF16) + _nbytes((T, HEAD_DIM), F32)
            + 3 * _nbytes((T, HEAD_DIM), BF16))
    scratch = 3 * _nbytes((T, HEAD_DIM), BF16) + 3 * _nbytes((T, HEAD_DIM), F32)
    return pl.pallas_call(
        body, name=name, grid=(H,),
        in_specs=_qkv_specs(T, ns, H) + [gspec, gspec, hspec, hspec],
        out_specs=[pl.BlockSpec((3, T, HEAD_DIM), lambda h: (0, 0, h)), gspec, gspec],
        out_shape=[jax.ShapeDtypeStruct((3, T, D), BF16), jax.ShapeDtypeStruct((1, HEAD_DIM), F32),
                   jax.ShapeDtypeStruct((1, HEAD_DIM), F32)],
        scratch_shapes=[pltpu.VMEM((T, HEAD_DIM), BF16)] * 3 + [pltpu.VMEM((T, HEAD_DIM), F32)] * 3,
        compiler_params=_params(step, scratch + 12 * _nbytes((BQ, BQ), F32)),
    )(qkv, qkv, qkv, gq, gk, do16, o32)


def _by_group(g, vals):
    out = vals[-1]
    for k in range(len(vals) - 2, -1, -1):
        out = jnp.where(g == k, vals[k], out)
    return out


def _pool_fwd(name, hf, x, wp, scale_row):
    T, D = x.shape
    G = len(POOL_WINDOWS)
    C = D // G
    tm = _tile(T, 512, POOL_HALO)
    hb = tm // POOL_HALO

    def body(h_ref, halo_ref, x_ref, w_ref, s_ref, xo_ref, p_ref):
        g, i = pl.program_id(0), pl.program_id(1)
        hv = h_ref[...]
        ext = jnp.concatenate([halo_ref[...] * (i > 0).astype(F32), hv], axis=0)
        sums, acc = [], ext
        for k in range(len(POOL_WINDOWS)):
            acc = acc + pltpu.roll(acc, 1 << k, 0)
            sums.append(acc)
        ws = _by_group(g, sums)[POOL_HALO:]
        t = i * tm + lax.broadcasted_iota(jnp.int32, (tm, 1), 0)
        cnt = jnp.minimum(t + 1, lax.shift_left(jnp.int32(2), g)).astype(F32)
        p = (ws / cnt - hv).astype(BF16)
        p_ref[...] = p
        xo_ref[...] = x_ref[...] + _dot(p, w_ref[...], NN) * s_ref[...]

    blk = pl.BlockSpec((tm, C), lambda g, i: (i, g))
    step = 3 * _nbytes((tm, C), F32) + _nbytes((tm, C), BF16) + _nbytes((C, C), BF16)
    return pl.pallas_call(
        body, name=name, grid=(G, T // tm),
        in_specs=[blk, pl.BlockSpec((POOL_HALO, C), lambda g, i: (jnp.maximum(i * hb - 1, 0), g)), blk,
                  pl.BlockSpec((None, C, C), lambda g, i: (g, 0, 0)), pl.BlockSpec((1, C), lambda g, i: (0, g))],
        out_specs=[blk, blk],
        out_shape=[jax.ShapeDtypeStruct((T, D), F32), jax.ShapeDtypeStruct((T, D), BF16)],
        compiler_params=_params(step + 6 * _nbytes((tm, C), F32)),
    )(hf, hf, x, wp, scale_row)


def _pool_bwd(name, dx, p, wp, scale_row):
    T, D = dx.shape
    G = len(POOL_WINDOWS)
    C = D // G
    tm = _tile(T, 512, POOL_HALO)
    hb = tm // POOL_HALO
    nt = T // tm
    n = tm + POOL_HALO

    def body(dx_ref, halo_ref, p_ref, w_ref, s_ref, dh_ref, dw_ref, ds_ref):
        g, i = pl.program_id(0), pl.program_id(1)
        dxv = dx_ref[...]
        dxe = jnp.concatenate([dxv, halo_ref[...] * (i < nt - 1).astype(F32)], axis=0)
        dyp = (dxe * s_ref[...]).astype(BF16)
        wv = w_ref[...]
        dp = _dot(dyp, wv, NT)
        t = i * tm + lax.broadcasted_iota(jnp.int32, (n, 1), 0)
        cnt = jnp.minimum(t + 1, lax.shift_left(jnp.int32(2), g)).astype(F32)
        sums, acc = [], dp / cnt
        for k in range(len(POOL_WINDOWS)):
            acc = acc + pltpu.roll(acc, n - (1 << k), 0)
            sums.append(acc)
        dh_ref[...] = _by_group(g, sums)[:tm] - dp[:tm]
        pv = p_ref[...]

        @pl.when(i == 0)
        def _():
            dw_ref[...] = jnp.zeros_like(dw_ref)
            ds_ref[...] = jnp.zeros_like(ds_ref)

        ds_ref[...] += jnp.sum(dxv * _dot(pv, wv, NN), axis=0, keepdims=True)
        dw_ref[...] += _dot(pv, dyp[:tm], TN)

    blk = pl.BlockSpec((tm, C), lambda g, i: (i, g))
    step = 2 * _nbytes((tm, C), F32) + _nbytes((tm, C), BF16) + _nbytes((C, C), BF16) + _nbytes((C, C), F32)
    return pl.pallas_call(
        body, name=name, grid=(G, nt),
        in_specs=[blk, pl.BlockSpec((POOL_HALO, C), lambda g, i: (jnp.minimum((i + 1) * hb, T // POOL_HALO - 1), g)),
                  blk, pl.BlockSpec((None, C, C), lambda g, i: (g, 0, 0)), pl.BlockSpec((1, C), lambda g, i: (0, g))],
        out_specs=[blk, pl.BlockSpec((None, C, C), lambda g, i: (g, 0, 0)), pl.BlockSpec((1, C), lambda g, i: (0, g))],
        out_shape=[jax.ShapeDtypeStruct((T, D), F32), jax.ShapeDtypeStruct((G, C, C), F32),
                   jax.ShapeDtypeStruct((1, D), F32)],
        compiler_params=_params(step + 8 * _nbytes((tm, C), F32)),
    )(dx, dx, p, wp, scale_row)


def _section_spec(rows, cw, ns, D, sec, row_map):
    per = ns // cw

    def imap(j, i):
        c = (sec * D) // cw + j
        return (c // per, row_map(i), c % per)

    return pl.BlockSpec((None, rows, cw), imap)


def _conv_fwd(name, bcx, cw_full):
    _, T, ns = bcx.shape
    D = cw_full.shape[1]
    cw = 256 if (ns % 256 == 0 and D % 256 == 0) else LANES
    tm = _tile(T, 512, CONV_HALO)
    hb = tm // CONV_HALO

    def body(b_ref, c_ref, u_ref, ch_ref, uh_ref, w_ref, o_ref):
        i = pl.program_id(1)
        gm = c_ref[...] * u_ref[...]
        ext = jnp.concatenate([ch_ref[...] * uh_ref[...] * (i > 0).astype(F32), gm], axis=0)
        w0, w1, w2 = w_ref[0:1, :], w_ref[1:2, :], w_ref[2:3, :]
        y = w2 * gm + w1 * pltpu.roll(ext, 1, 0)[CONV_HALO:] + w0 * pltpu.roll(ext, 2, 0)[CONV_HALO:]
        o_ref[...] = (b_ref[...] * y).astype(BF16)

    main = lambda i: i
    prev = lambda i: jnp.maximum(i * hb - 1, 0)
    return pl.pallas_call(
        body, name=name, grid=(D // cw, T // tm),
        in_specs=[_section_spec(tm, cw, ns, D, 0, main), _section_spec(tm, cw, ns, D, 1, main),
                  _section_spec(tm, cw, ns, D, 2, main), _section_spec(CONV_HALO, cw, ns, D, 1, prev),
                  _section_spec(CONV_HALO, cw, ns, D, 2, prev), pl.BlockSpec((3, cw), lambda j, i: (0, j))],
        out_specs=pl.BlockSpec((tm, cw), lambda j, i: (i, j)),
        out_shape=jax.ShapeDtypeStruct((T, D), BF16),
        compiler_params=_params(8 * _nbytes((tm, cw), F32)),
    )(bcx, bcx, bcx, bcx, bcx, cw_full)


def _conv_bwd(name, dby, bcx, cw_full):
    _, T, ns = bcx.shape
    D = cw_full.shape[1]
    cw = 256 if (ns % 256 == 0 and D % 256 == 0) else LANES
    tm = _tile(T, 512, CONV_HALO)
    hb = tm // CONV_HALO
    nt = T // tm
    n = tm + CONV_HALO

    def body(dby_ref, dbyh_ref, b_ref, bh_ref, c_ref, u_ref, ch_ref, uh_ref, w_ref, d3_ref, dw_ref):
        i = pl.program_id(1)
        w0, w1, w2 = w_ref[0:1, :], w_ref[1:2, :], w_ref[2:3, :]
        bv, cv, uv, dbyv = b_ref[...], c_ref[...], u_ref[...], dby_ref[...]
        gm = cv * uv
        ext_g = jnp.concatenate([ch_ref[...] * uh_ref[...] * (i > 0).astype(F32), gm], axis=0)
        g1 = pltpu.roll(ext_g, 1, 0)[CONV_HALO:]
        g2 = pltpu.roll(ext_g, 2, 0)[CONV_HALO:]
        y = w2 * gm + w1 * g1 + w0 * g2
        dy = dbyv * bv
        ext_dy = jnp.concatenate([dy, dbyh_ref[...] * bh_ref[...] * (i < nt - 1).astype(F32)], axis=0)
        dg = w2 * dy + w1 * pltpu.roll(ext_dy, n - 1, 0)[:tm] + w0 * pltpu.roll(ext_dy, n - 2, 0)[:tm]
        d3_ref[0] = (dbyv * y).astype(BF16)
        d3_ref[1] = (dg * uv).astype(BF16)
        d3_ref[2] = (dg * cv).astype(BF16)

        @pl.when(i == 0)
        def _():
            dw_ref[...] = jnp.zeros_like(dw_ref)

        rows = [jnp.sum(dy * v, axis=0, keepdims=True) for v in (g2, g1, gm)]
        dw_ref[...] += jnp.concatenate(rows + [jnp.zeros((8 - len(rows), cw), F32)], axis=0)

    main = lambda i: i
    prev = lambda i: jnp.maximum(i * hb - 1, 0)
    nxt = lambda i: jnp.minimum((i + 1) * hb, T // CONV_HALO - 1)
    return pl.pallas_call(
        body, name=name, grid=(D // cw, nt),
        in_specs=[pl.BlockSpec((tm, cw), lambda j, i: (i, j)), pl.BlockSpec((CONV_HALO, cw), lambda j, i: (nxt(i), j)),
                  _section_spec(tm, cw, ns, D, 0, main), _section_spec(CONV_HALO, cw, ns, D, 0, nxt),
                  _section_spec(tm, cw, ns, D, 1, main), _section_spec(tm, cw, ns, D, 2, main),
                  _section_spec(CONV_HALO, cw, ns, D, 1, prev), _section_spec(CONV_HALO, cw, ns, D, 2, prev),
                  pl.BlockSpec((3, cw), lambda j, i: (0, j))],
        out_specs=[pl.BlockSpec((3, tm, cw), lambda j, i: (0, i, j)), pl.BlockSpec((8, cw), lambda j, i: (0, j))],
        out_shape=[jax.ShapeDtypeStruct((3, T, D), BF16), jax.ShapeDtypeStruct((8, D), F32)],
        compiler_params=_params(14 * _nbytes((tm, cw), F32)),
    )(dby, dby, bcx, bcx, bcx, bcx, bcx, bcx, cw_full)


def _position():
    return lax.axis_index("x"), lax.axis_index("y"), lax.axis_index("c")


def _all_gather(name, tensors):
    n = len(tensors)
    shapes = [a.shape[1:] if idx is not None else a.shape for a, idx in tensors]

    def body(*refs):
        srcs, outs = refs[:n], refs[n:2 * n]
        send_sems, recv_sems, local_sems = refs[2 * n:]
        x, y, c = _position()
        me, sibling = (x, y, c), (x, y, 1 - c)
        chips = [(1 - x, y), (x, 1 - y), (1 - x, 1 - y)]

        def slot(out, p):
            return out.at[4 * p[0] + 2 * p[1] + p[2]]

        def copy(t, k, block, to, src=None):
            return pltpu.make_async_remote_copy(
                src_ref=slot(outs[t], block) if src is None else src, dst_ref=slot(outs[t], block),
                send_sem=send_sems.at[7 * t + k], recv_sem=recv_sems.at[7 * t + k],
                device_id=to, device_id_type=MESH)

        started, mine = [], []
        for t, (_, idx) in enumerate(tensors):
            src = srcs[t] if idx is None else srcs[t].at[idx]
            own = pltpu.make_async_copy(src, slot(outs[t], me), local_sems.at[t])
            own.start()
            mine.append(own)
            first = [copy(t, 0, me, sibling, src=src)]
            first += [copy(t, 1 + j, me, (*chip, c), src=src) for j, chip in enumerate(chips)]
            for cp in first:
                cp.start()
            started += first
        for t in range(n):
            for j, chip in enumerate(chips):
                copy(t, 1 + j, (*chip, c), me).wait_recv()
                passed = copy(t, 4 + j, (*chip, c), sibling)
                passed.start()
                started.append(passed)
        for t in range(n):
            copy(t, 0, sibling, me).wait_recv()
            for j, chip in enumerate(chips):
                copy(t, 4 + j, (*chip, 1 - c), me).wait_recv()
        for cp in started:
            cp.wait_send()
        for own in mine:
            own.wait()

    return pl.pallas_call(
        body, name=name,
        in_specs=[ANY] * n, out_specs=[ANY] * n,
        out_shape=[jax.ShapeDtypeStruct((N_DEV,) + tuple(s), a.dtype) for s, (a, _) in zip(shapes, tensors)],
        scratch_shapes=[pltpu.SemaphoreType.DMA((7 * n,)), pltpu.SemaphoreType.DMA((7 * n,)),
                        pltpu.SemaphoreType.DMA((n,))],
    )(*[a for a, _ in tensors])


def _core_copies(n):
    def copies(srcs, lands, send_sems, recv_sems):
        x, y, c = _position()
        out = []
        for t in range(n):
            for k in range(N_XY):
                out.append(pltpu.make_async_remote_copy(
                    src_ref=srcs[t].at[2 * k + 1 - c], dst_ref=lands[t].at[k],
                    send_sem=send_sems.at[N_XY * t + k], recv_sem=recv_sems.at[N_XY * t + k],
                    device_id=(x, y, 1 - c), device_id_type=MESH))
        return out

    return copies, N_XY * n


def _chip_partial(name, g, got, core):
    _, R, C = g.shape
    tr = _row_tile(R, C, 4, 2 << 20)

    def body(c_ref, a_ref, b_ref, o_ref):
        o_ref[...] = (a_ref[...] + b_ref[...]).astype(BF16)

    blk = pl.BlockSpec((None, tr, C), lambda k, i, c: (k, i, 0))
    return pl.pallas_call(
        body, name=name,
        grid_spec=pltpu.PrefetchScalarGridSpec(
            num_scalar_prefetch=1, grid=(N_XY, R // tr),
            in_specs=[pl.BlockSpec((None, tr, C), lambda k, i, c: (2 * k + c[0], i, 0)), blk], out_specs=blk),
        out_shape=jax.ShapeDtypeStruct(got.shape, BF16),
        compiler_params=_params(3 * _nbytes((tr, C), F32)),
    )(core, g, got)


def _chip_copies(n):
    def copies(srcs, lands, send_sems, recv_sems):
        x, y, c = _position()
        mine = 2 * x + y
        out = []
        for t in range(n):
            for d in range(1, N_XY):
                px, py = x ^ (d >> 1), y ^ (d & 1)
                out.append(pltpu.make_async_remote_copy(
                    src_ref=srcs[t].at[2 * px + py], dst_ref=lands[t].at[mine],
                    send_sem=send_sems.at[3 * t + d - 1], recv_sem=recv_sems.at[3 * t + d - 1],
                    device_id=(px, py, c), device_id_type=MESH))
        return out

    return copies, 3 * n


def _gather_copies(tensors):
    def copies(srcs, lands, send_sems, recv_sems):
        x, y, c = _position()
        me = 4 * x + 2 * y + c
        out = []
        for t, (_, idx) in enumerate(tensors):
            for d in range(1, N_DEV):
                out.append(pltpu.make_async_remote_copy(
                    src_ref=srcs[t].at[idx], dst_ref=lands[t].at[me],
                    send_sem=send_sems.at[7 * t + d - 1], recv_sem=recv_sems.at[7 * t + d - 1],
                    device_id=(x ^ (d >> 2), y ^ ((d >> 1) & 1), c ^ (d & 1)), device_id_type=MESH))
        return out

    return copies, 7 * len(tensors)


HBM = pl.BlockSpec(memory_space=pltpu.HBM)
SEM = pl.BlockSpec(memory_space=pltpu.SEMAPHORE)
EFFECT = pltpu.SideEffectType.DATAFLOW_SIDE_EFFECTING


def _exchange_start(name, exchange, srcs, lands, after):
    copies, n_sem = exchange
    n = len(srcs)

    def body(*refs):
        for cp in copies(refs[:n], refs[n:2 * n], refs[2 * n + 1], refs[2 * n + 2]):
            cp.start()
        refs[-1][...] = jnp.zeros_like(refs[-1])

    operands = [pltpu.with_memory_space_constraint(a, pltpu.HBM) for a in list(srcs) + list(lands)]
    outs = pl.pallas_call(
        body, name=name, in_specs=[HBM] * (2 * n) + [ANY],
        out_specs=[SEM, SEM] + [HBM] * n + [pl.BlockSpec(memory_space=pltpu.VMEM)],
        out_shape=[pltpu.SemaphoreType.DMA((n_sem,)), pltpu.SemaphoreType.DMA((n_sem,))]
        + [pltpu.HBM(a.shape, a.dtype) for a in lands] + [jax.ShapeDtypeStruct((8, LANES), F32)],
        input_output_aliases={n + k: 2 + k for k in range(n)},
        compiler_params=pltpu.CompilerParams(has_side_effects=EFFECT),
    )(*operands, after)
    return outs[0], outs[1], outs[2:2 + n], outs[-1]


def _exchange_wait(name, exchange, send_sems, recv_sems, srcs, lands, after):
    copies, _ = exchange
    n = len(srcs)

    def body(*refs):
        for cp in copies(refs[:n], refs[n:2 * n], refs[2 * n], refs[2 * n + 1]):
            cp.wait_send()
            cp.wait_recv()

    return pl.pallas_call(
        body, name=name, in_specs=[HBM] * (2 * n) + [SEM, SEM, ANY], out_specs=[HBM] * n,
        out_shape=[pltpu.HBM(a.shape, a.dtype) for a in lands],
        input_output_aliases={n + k: k for k in range(n)},
        compiler_params=pltpu.CompilerParams(has_side_effects=EFFECT),
    )(*srcs, *lands, send_sems, recv_sems, after)


def _adamw(w, g, m, v):
    m = ADAM_B1 * m + (1.0 - ADAM_B1) * g
    v = ADAM_B2 * v + (1.0 - ADAM_B2) * (g * g)
    m_hat = m / (1.0 - ADAM_B1 ** ADAM_STEP)
    v_hat = v / (1.0 - ADAM_B2 ** ADAM_STEP)
    return -ADAM_LR * (m_hat / (jnp.sqrt(v_hat) + ADAM_EPS) + ADAM_WD * w), m, v


def _reduce_update(name, own, landed, chip, w, m, v, layer, prev):
    _, R, C = landed.shape
    L = w.shape[0]
    tr = _row_tile(R, C, 4, 1 << 20, 16)

    def body(chip_ref, own_ref, p_ref, w_ref, m_ref, v_ref, *rest):
        g_ref, d_ref, mo_ref, vo_ref = rest[-4:]
        mine = chip_ref[0]
        g = None
        for k in range(N_XY):
            term = jnp.where(mine == k, own_ref[...], p_ref[k]).astype(F32)
            g = term if g is None else g + term
        g_ref[...] = g
        d_ref[...], mo_ref[...], vo_ref[...] = _adamw(w_ref[...], g, m_ref[...], v_ref[...])

    lay = pl.BlockSpec((None, tr, C), lambda i, c: (layer, i, 0))
    osh = jax.ShapeDtypeStruct((L, R, C), F32)
    extra = [] if prev is None else list(prev)
    return pl.pallas_call(
        body, name=name,
        grid_spec=pltpu.PrefetchScalarGridSpec(
            num_scalar_prefetch=1, grid=(R // tr,),
            in_specs=[pl.BlockSpec((None, tr, C), lambda i, c: (c[0], i, 0)),
                      pl.BlockSpec((N_XY, tr, C), lambda i, c: (0, i, 0)), lay, lay, lay] + [ANY] * len(extra),
            out_specs=[lay] * 4),
        out_shape=[osh] * 4,
        input_output_aliases={6 + k: k for k in range(len(extra))},
        compiler_params=_params(9 * _nbytes((tr, C), F32)),
    )(chip, own, landed, w, m, v, *extra)


def _all_reduce_small(name, vec):
    R = vec.shape[0]

    def body(v_ref, o_ref, tot_ref, buf, send_sems, recv_sems):
        x, y, c = _position()
        me = 4 * x + 2 * y + c
        buf[me] = v_ref[...]
        sends = []
        for d in range(1, N_DEV):
            cp = pltpu.make_async_remote_copy(
                src_ref=v_ref, dst_ref=buf.at[me], send_sem=send_sems.at[d - 1], recv_sem=recv_sems.at[d - 1],
                device_id=(x ^ (d >> 2), y ^ ((d >> 1) & 1), c ^ (d & 1)), device_id_type=MESH)
            cp.start()
            sends.append(cp)
        for cp in sends:
            cp.wait()
        s = buf[0]
        for k in range(1, N_DEV):
            s = s + buf[k]
        o_ref[...] = s
        tot_ref[...] = jnp.sum(jnp.sum(s[0:8], axis=0, keepdims=True), axis=1, keepdims=True)

    vm = pl.BlockSpec(memory_space=pltpu.VMEM)
    return pl.pallas_call(
        body, name=name, in_specs=[vm], out_specs=[vm, vm],
        out_shape=[jax.ShapeDtypeStruct(vec.shape, F32), jax.ShapeDtypeStruct((1, 1), F32)],
        scratch_shapes=[pltpu.VMEM((N_DEV, R, LANES), F32), pltpu.SemaphoreType.DMA((N_DEV - 1,)),
                        pltpu.SemaphoreType.DMA((N_DEV - 1,))],
    )(vec)


def _adamw_small(name, w, g, m, v):
    def body(w_ref, g_ref, m_ref, v_ref, d_ref, mo_ref, vo_ref):
        d_ref[...], mo_ref[...], vo_ref[...] = _adamw(w_ref[...], g_ref[...], m_ref[...], v_ref[...])

    vm = pl.BlockSpec(memory_space=pltpu.VMEM)
    return pl.pallas_call(
        body, name=name, in_specs=[vm] * 4, out_specs=[vm] * 3,
        out_shape=[jax.ShapeDtypeStruct(w.shape, F32)] * 3,
    )(w, g, m, v)


def _pack(parts, rows):
    flat = jnp.concatenate([p.reshape(-1) for p in parts])
    return jnp.pad(flat, (0, rows * LANES - flat.shape[0])).reshape(rows, LANES)


def _unpack(packed, shapes, skip=0):
    flat, out, off = packed.reshape(-1), [], skip
    for s in shapes:
        n = 1
        for d in s:
            n *= d
        out.append(flat[off:off + n].reshape(s))
        off += n
    return out


def kernel(x, norm_mix_g, norm_ffn_g, sb_w_qkv, sb_g_q, sb_g_k, sb_w_o, pool_w, pool_scale, conv_w_in, conv_w, conv_w_out, ffn_w_gate, ffn_w_up, ffn_w_down, loss_target, m_norm_mix_g, m_norm_ffn_g, m_sb_w_qkv, m_sb_g_q, m_sb_g_k, m_sb_w_o, m_pool_w, m_pool_scale, m_conv_w_in, m_conv_w, m_conv_w_out, m_ffn_w_gate, m_ffn_w_up, m_ffn_w_down, v_norm_mix_g, v_norm_ffn_g, v_sb_w_qkv, v_sb_g_q, v_sb_g_k, v_sb_w_o, v_pool_w, v_pool_scale, v_conv_w_in, v_conv_w, v_conv_w_out, v_ffn_w_gate, v_ffn_w_up, v_ffn_w_down):
    _, T, D = x.shape
    depth = norm_mix_g.shape[0]
    H = D // HEAD_DIM
    G = len(POOL_WINDOWS)
    pool_rows = pool_w.shape[2]
    xs = x[0]

    big = {
        "sb_w_qkv": (sb_w_qkv, m_sb_w_qkv, v_sb_w_qkv), "sb_w_o": (sb_w_o, m_sb_w_o, v_sb_w_o),
        "pool_w": tuple(a.reshape(a.shape[0], G * pool_rows, a.shape[3]) for a in (pool_w, m_pool_w, v_pool_w)),
        "conv_w_in": (conv_w_in, m_conv_w_in, v_conv_w_in), "conv_w_out": (conv_w_out, m_conv_w_out, v_conv_w_out),
        "ffn_w_gate": (ffn_w_gate, m_ffn_w_gate, v_ffn_w_gate), "ffn_w_up": (ffn_w_up, m_ffn_w_up, v_ffn_w_up),
        "ffn_w_down": (ffn_w_down, m_ffn_w_down, v_ffn_w_down),
    }
    w16 = {k: t[0].astype(BF16) for k, t in big.items()}

    me = 4 * lax.axis_index("x") + 2 * lax.axis_index("y") + lax.axis_index("c")

    def layer_tensors(i):
        kind, j = i % 3, i // 3
        names = list([("sb_w_qkv", "sb_w_o"), ("pool_w",), ("conv_w_in", "conv_w_out")][kind])
        tensors = [(w16[nm], j) for nm in names] + [(w16[nm], i) for nm in ("ffn_w_gate", "ffn_w_up", "ffn_w_down")]
        names += ["ffn_w_gate", "ffn_w_up", "ffn_w_down"]
        if kind == 2:
            tensors.append((conv_w, j))
            names.append("conv_w")
        return names, tensors

    def gather_start(i, after):
        names, tensors = layer_tensors(i)
        lands = [lax.dynamic_update_slice(lax.empty((N_DEV,) + a.shape[1:], a.dtype), a[idx][None], (me, 0, 0))
                 for a, idx in tensors]
        exchange, srcs = _gather_copies(tensors), [a for a, _ in tensors]
        return (names, exchange, srcs) + _exchange_start(f"gather_start_l{i}", exchange, srcs, lands, after)

    def gather_wait(i, pending, after):
        names, exchange, srcs, send_sems, recv_sems, lands, _ = pending
        return dict(zip(names, _exchange_wait(f"gather_wait_l{i}", exchange, send_sems, recv_sems, srcs, lands, after)))

    names0, tensors0 = layer_tensors(0)
    gathered = [dict(zip(names0, _all_gather("gather_l0", tensors0)))] + [None] * (depth - 1)

    def std_cols(wb):
        return jnp.transpose(wb, (1, 0, 2)).reshape(wb.shape[1], -1)

    saved = []
    xc = xs
    in_flight = {}
    for i in range(depth):
        kind, j = i % 3, i // 3
        if i in in_flight:
            gathered[i] = gather_wait(i, in_flight.pop(i), xc)
        gw = gathered[i]
        for nxt in (range(1, depth) if i == 0 else ()):
            in_flight[nxt] = gather_start(nxt, gw["ffn_w_down"] if nxt == 1 else in_flight[nxt - 1][-1])
        follows = tuple(p[-1] for p in in_flight.values())
        s = {"x_in": xc}
        if kind == 0:
            h = _rms_fwd(f"norm_mix_l{i}", xc, norm_mix_g[i:i + 1], BF16, follows)
            qkv = _mm_cols(f"qkv_l{i}", h, gw["sb_w_qkv"], F32)
            o32, o16 = _attn_fwd(f"attn_fwd_l{i}", qkv, sb_g_q[j:j + 1], sb_g_k[j:j + 1], H)
            xc = _mm_res(f"attn_out_l{i}", o16, gw["sb_w_o"].reshape(D, D), xc)
            s.update(h=h, qkv=qkv, o32=o32, o16=o16)
        elif kind == 1:
            hf = _rms_fwd(f"norm_mix_l{i}", xc, norm_mix_g[i:i + 1], F32, follows)
            wp = jnp.transpose(gw["pool_w"].reshape(N_DEV, G, pool_rows, D // G), (1, 0, 2, 3)).reshape(G, D // G, D // G)
            xc, p = _pool_fwd(f"pool_fwd_l{i}", hf, xc, wp, pool_scale[j:j + 1])
            s.update(p=p, wp=wp)
        else:
            h = _rms_fwd(f"norm_mix_l{i}", xc, norm_mix_g[i:i + 1], BF16, follows)
            bcx =_mm_cols(f"conv_in_l{i}", h, gw["conv_w_in"], F32)
            cw_full = jnp.transpose(gw["conv_w"], (1, 0, 2)).reshape(3, D)
            by = _conv_fwd(f"conv_fwd_l{i}", bcx, cw_full)
            xc = _mm_res(f"conv_out_l{i}", by, gw["conv_w_out"].reshape(D, D), xc)
            s.update(h=h, bcx=bcx, by=by, cw_full=cw_full)
        s["x_mid"] = xc
        h2 = _rms_fwd(f"norm_ffn_l{i}", xc, norm_ffn_g[i:i + 1], BF16)
        gate, up, act = _ffn_up(f"ffn_up_l{i}", h2, gw["ffn_w_gate"], gw["ffn_w_up"])
        xc = _ffn_down(f"ffn_down_l{i}", act, gw["ffn_w_down"], xc)
        s.update(h2=h2, gate=gate, up=up, act=act)
        saved.append(s)

    dx, dx16, loss_part = _loss_head("loss_head", xc, loss_target[0])

    outs = {k: None for k in big}
    core = lax.axis_index("c").astype(jnp.int32).reshape(1)
    chip = (2 * lax.axis_index("x") + lax.axis_index("y")).astype(jnp.int32).reshape(1)

    def core_start(tag, grads):
        names = list(grads)
        arrays = [grads[nm][0] for nm in names]
        exchange = _core_copies(len(names))
        lands = [lax.empty((N_XY,) + g.shape[1:], g.dtype) for g in arrays]
        started = _exchange_start(f"swap_core_start_{tag}", exchange, arrays, lands, arrays[0])
        return (tag, [(nm, grads[nm][1]) for nm in names], exchange, arrays) + started

    def chip_start(pending, after):
        tag, layers, exchange, arrays, send_sems, recv_sems, lands, _ = pending
        got = _exchange_wait(f"swap_core_wait_{tag}", exchange, send_sems, recv_sems, arrays, lands, after)
        partial = [_chip_partial(f"chip_sum_{nm}_{tag}", g, b, core) for (nm, _), g, b in zip(layers, arrays, got)]
        exchange = _chip_copies(len(layers))
        lands = [lax.empty(p.shape, p.dtype) for p in partial]
        started = _exchange_start(f"swap_chip_start_{tag}", exchange, partial, lands, got[0])
        return (tag, layers, exchange, partial) + started

    def reduce_finish(pending, after):
        tag, layers, exchange, srcs, send_sems, recv_sems, lands, _ = pending
        lands = _exchange_wait(f"swap_chip_wait_{tag}", exchange, send_sems, recv_sems, srcs, lands, after)
        for (nm, layer), own, landed in zip(layers, srcs, lands):
            w, m, v = big[nm]
            outs[nm] = _reduce_update(f"update_{nm}_{tag}", own, landed, chip, w, m, v, layer, outs[nm])

    def behind(small_operand, token):
        return small_operand + token[0, 0]

    d_mix, d_ffn = [None] * depth, [None] * depth
    small = {}
    mix_core = None
    for i in reversed(range(depth)):
        kind, j = i % 3, i // 3
        gw, s = gathered[i], saved[i]
        grads = {}
        dgate, dup = _ffn_dact(f"ffn_dact_l{i}", dx16, gw["ffn_w_down"], s["gate"], s["up"],
                               () if mix_core is None else (mix_core[-1],))
        dwd = _wgrad_rows(f"ffn_dwd_l{i}", s["act"], dx16)
        mix_chip = None if mix_core is None else chip_start(mix_core, dwd)
        dh2 = _ffn_dh(f"ffn_dh_l{i}", dgate, dup, gw["ffn_w_gate"], gw["ffn_w_up"],
                      () if mix_chip is None else (mix_chip[-1],))
        dwg, dwu = _wgrad_cols(f"ffn_dwgu_l{i}", s["h2"], [dgate, dup])
        if mix_chip is not None:
            reduce_finish(mix_chip, dwu)
        ffn_core = core_start(f"ffn_l{i}", {"ffn_w_down": (dwd, i), "ffn_w_gate": (dwg, i), "ffn_w_up": (dwu, i)})
        dx, dx16, d_ffn[i] = _rms_bwd(f"norm_ffn_bwd_l{i}", dh2, s["x_mid"], norm_ffn_g[i:i + 1], dx,
                                      (ffn_core[-1],))
        g_mix_row = norm_mix_g[i:i + 1]
        if kind == 0:
            wo = gw["sb_w_o"]
            do16 = _mm_nt_rows(f"attn_do_l{i}", dx16, wo, BF16)
            dwo = _wgrad_std(f"attn_dwo_l{i}", s["o16"], dx16)
            grads["sb_w_o"] = (dwo.reshape(wo.shape), j)
            ffn_chip = chip_start(ffn_core, dwo)
            d3, dgq, dgk = _attn_bwd(f"attn_bwd_l{i}", s["qkv"], behind(sb_g_q[j:j + 1], ffn_chip[-1]),
                                     sb_g_k[j:j + 1], do16, s["o32"], H)
            small[("sb_g_q", j)], small[("sb_g_k", j)] = dgq, dgk
            dh = _mm_nt_sections(f"qkv_dh_l{i}", d3, std_cols(gw["sb_w_qkv"]))
            grads["sb_w_qkv"] = (_wgrad_sections(f"qkv_dw_l{i}", s["h"], d3, gw["sb_w_qkv"].shape[2]), j)
        elif kind == 1:
            dh, dwp, dps = _pool_bwd(f"pool_bwd_l{i}", dx, s["p"], s["wp"], pool_scale[j:j + 1])
            small[("pool_scale", j)] = dps
            dwp = jnp.transpose(dwp.reshape(G, N_DEV, pool_rows, D // G), (1, 0, 2, 3)).reshape(N_DEV, G * pool_rows, D // G)
            grads["pool_w"] = (dwp, j)
            ffn_chip = chip_start(ffn_core, dps)
            g_mix_row = behind(g_mix_row, ffn_chip[-1])
        else:
            wout = gw["conv_w_out"]
            dby = _mm_nt_rows(f"conv_dby_l{i}", dx16, wout, F32)
            dwout = _wgrad_std(f"conv_dwout_l{i}", s["by"], dx16)
            grads["conv_w_out"] = (dwout.reshape(wout.shape), j)
            ffn_chip = chip_start(ffn_core, dwout)
            d3, dcw = _conv_bwd(f"conv_bwd_l{i}", dby, s["bcx"], behind(s["cw_full"], ffn_chip[-1]))
            small[("conv_w", j)] = dcw[0:3]
            dh = _mm_nt_sections(f"conv_dh_l{i}", d3, std_cols(gw["conv_w_in"]))
            grads["conv_w_in"] = (_wgrad_sections(f"conv_dwin_l{i}", s["h"], d3, gw["conv_w_in"].shape[2]), j)
        dx, dx16, d_mix[i] = _rms_bwd(f"norm_mix_bwd_l{i}", dh, s["x_in"], g_mix_row, dx)
        reduce_finish(ffn_chip, d_mix[i])
        mix_core = core_start(f"mix_l{i}", grads)
    mix_chip = chip_start(mix_core, mix_core[-1])

    n_sb, n_pool, n_conv = sb_g_q.shape[0], pool_scale.shape[0], conv_w.shape[0]
    pieces = [loss_part, jnp.concatenate(d_mix), jnp.concatenate(d_ffn),
              jnp.concatenate([small[("sb_g_q", j)] for j in range(n_sb)]),
              jnp.concatenate([small[("sb_g_k", j)] for j in range(n_sb)]),
              jnp.concatenate([small[("pool_scale", j)] for j in range(n_pool)]),
              jnp.stack([small[("conv_w", j)] for j in range(n_conv)])]
    n_small = sum(p.size for p in pieces)
    rows = -(-n_small // (8 * LANES)) * 8
    summed, loss = _all_reduce_small("reduce_small", _pack(pieces, rows))
    shapes = [norm_mix_g.shape, norm_ffn_g.shape, sb_g_q.shape, sb_g_k.shape, pool_scale.shape, (n_conv, 3, D)]
    g_mix, g_ffn, g_q, g_k, g_ps, g_cw_full = _unpack(summed, shapes, skip=8 * LANES)
    cshard = conv_w.shape[2]
    me = 4 * lax.axis_index("x") + 2 * lax.axis_index("y") + lax.axis_index("c")
    g_cw = lax.dynamic_slice_in_dim(g_cw_full, me * cshard, cshard, axis=2)
    small_names = ["norm_mix_g", "norm_ffn_g", "sb_g_q", "sb_g_k", "pool_scale", "conv_w"]
    small_w = [norm_mix_g, norm_ffn_g, sb_g_q, sb_g_k, pool_scale, conv_w]
    small_m = [m_norm_mix_g, m_norm_ffn_g, m_sb_g_q, m_sb_g_k, m_pool_scale, m_conv_w]
    small_v = [v_norm_mix_g, v_norm_ffn_g, v_sb_g_q, v_sb_g_k, v_pool_scale, v_conv_w]
    small_g = [g_mix, g_ffn, g_q, g_k, g_ps, g_cw]
    n_upd = sum(a.size for a in small_w)
    urows = -(-n_upd // (8 * LANES)) * 8
    sd, sm, sv = _adamw_small("update_small", _pack(small_w, urows), _pack(small_g, urows),
                              _pack(small_m, urows), _pack(small_v, urows))
    reduce_finish(mix_chip, sd)
    sshapes = [a.shape for a in small_w]
    res = {nm: (g, d, m_, v_) for nm, g, d, m_, v_ in
           zip(small_names, small_g, _unpack(sd, sshapes), _unpack(sm, sshapes), _unpack(sv, sshapes))}
    for nm, stacks in outs.items():
        shape = pool_w.shape if nm == "pool_w" else big[nm][0].shape
        res[nm] = tuple(a.reshape(shape) for a in stacks)

    order = ["norm_mix_g", "norm_ffn_g", "sb_w_qkv", "sb_g_q", "sb_g_k", "sb_w_o", "pool_w", "pool_scale",
             "conv_w_in", "conv_w", "conv_w_out", "ffn_w_gate", "ffn_w_up", "ffn_w_down"]
    return (loss.reshape(()), dx[None], *[res[nm][0] for nm in order], *[res[nm][1] for nm in order],
            *[res[nm][2] for nm in order], *[res[nm][3] for nm in order])
```

```python
import functools

import jax
import jax.numpy as jnp
from jax import lax
from jax.experimental import pallas as pl
from jax.experimental.pallas import tpu as pltpu

F32 = jnp.float32
BF16 = jnp.bfloat16

N_DEV = 8
N_XY = 4
HEAD_DIM = 128
LANES = 128
POOL_WINDOWS = (2, 4, 8, 16)
POOL_HALO = 16
CONV_HALO = 8
ATTN_BLOCK = 512
SHARDS_PER_STEP = 2
WGRAD_TOKENS = 2048
WGRAD_TOKENS_NARROW = 4096
EPS = 1e-6
ADAM_LR = 0.001
ADAM_B1 = 0.9
ADAM_B2 = 0.999
ADAM_EPS = 1e-08
ADAM_WD = 0.01
ADAM_STEP = 10

VMEM_CAP_V7X = 56 * 1024 * 1024
VMEM_FLOOR = 32 * 1024 * 1024

NN = (((1,), (0,)), ((), ()))
NT = (((1,), (1,)), ((), ()))
TN = (((0,), (0,)), ((), ()))
MESH = pl.DeviceIdType.MESH
ANY = pl.BlockSpec(memory_space=pl.ANY)


def _dot(a, b, dims):
    return lax.dot_general(a, b, dims, preferred_element_type=F32)


def _params(step_bytes, scratch_bytes=0):
    need = 2 * step_bytes + scratch_bytes + 3 * step_bytes // 2 + (4 << 20)
    return pltpu.CompilerParams(vmem_limit_bytes=int(min(VMEM_CAP_V7X, max(VMEM_FLOOR, need))))


def _nbytes(shape, dtype):
    n = 1
    for s in shape:
        if s is not None:
            n *= s
    return n * jnp.dtype(dtype).itemsize


def _tile(n, target, mult=8):
    t = min(n, target)
    t -= t % mult
    while t > mult and n % t:
        t -= mult
    assert t >= mult and n % t == 0, (n, target, mult)
    return t


def _row_tile(rows, cols, itemsize, budget, mult=16):
    return _tile(rows, max(mult, budget // (cols * itemsize)), mult)


def _rms_fwd(name, x, g_row, out_dtype, follows=()):
    T, D = x.shape
    tm = _tile(T, 512)

    def body(x_ref, g_ref, *rest):
        o_ref = rest[-1]
        xv = x_ref[...]
        r = lax.rsqrt(jnp.mean(xv * xv, axis=-1, keepdims=True) + EPS)
        o_ref[...] = (xv * r * g_ref[...]).astype(o_ref.dtype)

    return pl.pallas_call(
        body, name=name, grid=(T // tm,),
        in_specs=[pl.BlockSpec((tm, D), lambda i: (i, 0)), pl.BlockSpec((1, D), lambda i: (0, 0))] + [ANY] * len(follows),
        out_specs=pl.BlockSpec((tm, D), lambda i: (i, 0)),
        out_shape=jax.ShapeDtypeStruct((T, D), out_dtype),
        compiler_params=_params(_nbytes((tm, D), F32) * 2),
    )(x, g_row, *follows)


def _rms_bwd(name, dh, x, g_row, dres, follows=()):
    T, D = x.shape
    tm = _tile(T, 256)

    def body(dh_ref, x_ref, g_ref, dres_ref, *rest):
        dx_ref, dx16_ref, dg_ref = rest[-3:]
        xv = x_ref[...]
        r = lax.rsqrt(jnp.mean(xv * xv, axis=-1, keepdims=True) + EPS)
        xh = xv * r
        dhv = dh_ref[...]
        dy = dhv * g_ref[...]
        m = jnp.mean(dy * xh, axis=-1, keepdims=True)
        dx = dres_ref[...] + r * (dy - xh * m)
        dx_ref[...] = dx
        dx16_ref[...] = dx.astype(BF16)

        @pl.when(pl.program_id(0) == 0)
        def _():
            dg_ref[...] = jnp.zeros_like(dg_ref)

        dg_ref[...] += jnp.sum(dhv * xh, axis=0, keepdims=True)

    blk = pl.BlockSpec((tm, D), lambda i: (i, 0))
    row = pl.BlockSpec((1, D), lambda i: (0, 0))
    return pl.pallas_call(
        body, name=name, grid=(T // tm,),
        in_specs=[blk, blk, row, blk] + [ANY] * len(follows), out_specs=[blk, blk, row],
        out_shape=[jax.ShapeDtypeStruct((T, D), F32), jax.ShapeDtypeStruct((T, D), BF16),
                   jax.ShapeDtypeStruct((1, D), F32)],
        compiler_params=_params(_nbytes((tm, D), F32) * 5),
    )(dh, x, g_row, dres, *follows)


def _loss_head(name, y, target):
    T, D = y.shape
    tm = _tile(T, 256)

    def body(y_ref, t_ref, dy_ref, dy16_ref, acc_ref):
        e = y_ref[...] - t_ref[...]
        d = e * (1.0 / D)
        dy_ref[...] = d
        dy16_ref[...] = d.astype(BF16)
        s = (e * e).reshape(tm // 8, 8, D).sum(axis=0)
        part = s[:, 0:LANES]
        for k in range(1, D // LANES):
            part = part + s[:, k * LANES:(k + 1) * LANES]

        @pl.when(pl.program_id(0) == 0)
        def _():
            acc_ref[...] = jnp.zeros_like(acc_ref)

        acc_ref[...] += part * (0.5 / D)

    blk = pl.BlockSpec((tm, D), lambda i: (i, 0))
    return pl.pallas_call(
        body, name=name, grid=(T // tm,),
        in_specs=[blk, blk], out_specs=[blk, blk, pl.BlockSpec((8, LANES), lambda i: (0, 0))],
        out_shape=[jax.ShapeDtypeStruct((T, D), F32), jax.ShapeDtypeStruct((T, D), BF16),
                   jax.ShapeDtypeStruct((8, LANES), F32)],
        compiler_params=_params(_nbytes((tm, D), F32) * 4),
    )(y, target)


def _mm_cols(name, a, w, out_dtype):
    T, K = a.shape
    nb, _, ns = w.shape
    tm = _tile(T, 1024)

    def body(a_ref, w_ref, o_ref):
        o_ref[...] = _dot(a_ref[...], w_ref[...], NN).astype(o_ref.dtype)

    step = _nbytes((tm, K), BF16) + _nbytes((K, ns), BF16) + _nbytes((tm, ns), out_dtype)
    return pl.pallas_call(
        body, name=name, grid=(nb, T // tm),
        in_specs=[pl.BlockSpec((tm, K), lambda b, i: (i, 0)),
                  pl.BlockSpec((None, K, ns), lambda b, i: (b, 0, 0))],
        out_specs=pl.BlockSpec((None, tm, ns), lambda b, i: (b, i, 0)),
        out_shape=jax.ShapeDtypeStruct((nb, T, ns), out_dtype),
        compiler_params=_params(step),
    )(a, w)


def _ffn_up(name, h, wg, wu):
    T, K = h.shape
    nb, _, fs = wg.shape
    tm = _tile(T, 1024)

    def body(a_ref, wg_ref, wu_ref, g_ref, u_ref, act_ref):
        av = a_ref[...]
        g = _dot(av, wg_ref[...], NN)
        u = _dot(av, wu_ref[...], NN)
        g_ref[...] = g
        u_ref[...] = u
        act_ref[...] = (g * jax.nn.sigmoid(g) * u).astype(BF16)

    wspec = pl.BlockSpec((None, K, fs), lambda b, i: (b, 0, 0))
    ospec = pl.BlockSpec((None, tm, fs), lambda b, i: (b, i, 0))
    osh = lambda dt: jax.ShapeDtypeStruct((nb, T, fs), dt)
    step = _nbytes((tm, K), BF16) + 2 * _nbytes((K, fs), BF16) + _nbytes((tm, fs), BF16) + 2 * _nbytes((tm, fs), F32)
    return pl.pallas_call(
        body, name=name, grid=(nb, T // tm),
        in_specs=[pl.BlockSpec((tm, K), lambda b, i: (i, 0)), wspec, wspec],
        out_specs=[ospec, ospec, ospec], out_shape=[osh(F32), osh(F32), osh(BF16)],
        compiler_params=_params(step),
    )(h, wg, wu)


def _mm_res(name, a, w, res):
    T, K = a.shape
    N = w.shape[1]
    tm, tn = _tile(T, 512), _tile(N, 1024, LANES)

    def body(a_ref, w_ref, r_ref, o_ref):
        o_ref[...] = r_ref[...] + _dot(a_ref[...], w_ref[...], NN)

    step = _nbytes((tm, K), BF16) + _nbytes((K, tn), BF16) + 2 * _nbytes((tm, tn), F32)
    return pl.pallas_call(
        body, name=name, grid=(N // tn, T // tm),
        in_specs=[pl.BlockSpec((tm, K), lambda j, i: (i, 0)), pl.BlockSpec((K, tn), lambda j, i: (0, j)),
                  pl.BlockSpec((tm, tn), lambda j, i: (i, j))],
        out_specs=pl.BlockSpec((tm, tn), lambda j, i: (i, j)),
        out_shape=jax.ShapeDtypeStruct((T, N), F32),
        compiler_params=_params(step),
    )(a, w, res)


def _ffn_down(name, act, wd, res):
    nb, T, fs = act.shape
    N = wd.shape[2]
    tm, tn = _tile(T, 1024), _tile(N, 1024, LANES)
    sp = SHARDS_PER_STEP
    steps = nb // sp

    def body(a_ref, w_ref, r_ref, o_ref, acc_ref):
        b = pl.program_id(2)

        @pl.when(b == 0)
        def _():
            acc_ref[...] = r_ref[...]

        acc_ref[...] += sum(_dot(a_ref[u], w_ref[u], NN) for u in range(sp))

        @pl.when(b == steps - 1)
        def _():
            o_ref[...] = acc_ref[...]

    step = sp * (_nbytes((tm, fs), BF16) + _nbytes((fs, tn), BF16)) + 2 * _nbytes((tm, tn), F32)
    return pl.pallas_call(
        body, name=name, grid=(T // tm, N // tn, steps),
        in_specs=[pl.BlockSpec((sp, tm, fs), lambda i, j, b: (b, i, 0)),
                  pl.BlockSpec((sp, fs, tn), lambda i, j, b: (b, 0, j)),
                  pl.BlockSpec((tm, tn), lambda i, j, b: (i, j))],
        out_specs=pl.BlockSpec((tm, tn), lambda i, j, b: (i, j)),
        out_shape=jax.ShapeDtypeStruct((T, N), F32),
        scratch_shapes=[pltpu.VMEM((tm, tn), F32)],
        compiler_params=_params(step, _nbytes((tm, tn), F32)),
    )(act, wd, res)


def _mm_nt_rows(name, g, w, out_dtype):
    T, N = g.shape
    nb, ks, _ = w.shape
    tm = _tile(T, 1024)

    def body(g_ref, w_ref, o_ref):
        o_ref[...] = _dot(g_ref[...], w_ref[...], NT).astype(o_ref.dtype)

    step = _nbytes((tm, N), BF16) + _nbytes((ks, N), BF16) + _nbytes((tm, ks), out_dtype)
    return pl.pallas_call(
        body, name=name, grid=(nb, T // tm),
        in_specs=[pl.BlockSpec((tm, N), lambda b, i: (i, 0)),
                  pl.BlockSpec((None, ks, N), lambda b, i: (b, 0, 0))],
        out_specs=pl.BlockSpec((tm, ks), lambda b, i: (i, b)),
        out_shape=jax.ShapeDtypeStruct((T, nb * ks), out_dtype),
        compiler_params=_params(step),
    )(g, w)


def _ffn_dact(name, dx16, wd, gate, up, follows=()):
    T, N = dx16.shape
    nb, fs, _ = wd.shape
    tm = _tile(T, 1024)

    def body(g_ref, w_ref, gate_ref, up_ref, *rest):
        dg_ref, du_ref = rest[-2:]
        dact = _dot(g_ref[...], w_ref[...], NT)
        gt = gate_ref[...]
        s = jax.nn.sigmoid(gt)
        silu = gt * s
        dg_ref[...] = (dact * up_ref[...] * (s * (1.0 + gt * (1.0 - s)))).astype(BF16)
        du_ref[...] = (dact * silu).astype(BF16)

    aspec = pl.BlockSpec((None, tm, fs), lambda b, i: (b, i, 0))
    osh = jax.ShapeDtypeStruct((nb, T, fs), BF16)
    step = _nbytes((tm, N), BF16) + _nbytes((fs, N), BF16) + 2 * _nbytes((tm, fs), BF16) + 3 * _nbytes((tm, fs), F32)
    return pl.pallas_call(
        body, name=name, grid=(nb, T // tm),
        in_specs=[pl.BlockSpec((tm, N), lambda b, i: (i, 0)),
                  pl.BlockSpec((None, fs, N), lambda b, i: (b, 0, 0)), aspec, aspec] + [ANY] * len(follows),
        out_specs=[aspec, aspec], out_shape=[osh, osh],
        compiler_params=_params(step),
    )(dx16, wd, gate, up, *follows)


def _ffn_dh(name, dgate, dup, wg, wu, follows=()):
    nb, T, fs = dgate.shape
    D = wg.shape[1]
    tm, tn = _tile(T, 1024), _tile(D, 1024, LANES)
    sp = SHARDS_PER_STEP
    steps = nb // sp

    def body(dg_ref, du_ref, wg_ref, wu_ref, *rest):
        o_ref, acc_ref = rest[-2:]
        b = pl.program_id(2)

        @pl.when(b == 0)
        def _():
            acc_ref[...] = jnp.zeros_like(acc_ref)

        acc_ref[...] += sum(_dot(dg_ref[u], wg_ref[u], NT) + _dot(du_ref[u], wu_ref[u], NT) for u in range(sp))

        @pl.when(b == steps - 1)
        def _():
            o_ref[...] = acc_ref[...]

    aspec = pl.BlockSpec((sp, tm, fs), lambda i, j, b: (b, i, 0))
    wspec = pl.BlockSpec((sp, tn, fs), lambda i, j, b: (b, j, 0))
    step = sp * (2 * _nbytes((tm, fs), BF16) + 2 * _nbytes((tn, fs), BF16)) + _nbytes((tm, tn), F32)
    return pl.pallas_call(
        body, name=name, grid=(T // tm, D // tn, steps),
        in_specs=[aspec, aspec, wspec, wspec] + [ANY] * len(follows),
        out_specs=pl.BlockSpec((tm, tn), lambda i, j, b: (i, j)),
        out_shape=jax.ShapeDtypeStruct((T, D), F32),
        scratch_shapes=[pltpu.VMEM((tm, tn), F32)],
        compiler_params=_params(step, _nbytes((tm, tn), F32)),
    )(dgate, dup, wg, wu, *follows)


def _mm_nt_sections(name, g3, w_std):
    ns_, T, Ds = g3.shape
    D = w_std.shape[0]
    tm, tn, tk = _tile(T, 1024), _tile(D, 1024, LANES), _tile(Ds, 2048, LANES)
    nkk = Ds // tk
    nk = ns_ * nkk

    def body(g_ref, w_ref, o_ref, acc_ref):
        r = pl.program_id(2)

        @pl.when(r == 0)
        def _():
            acc_ref[...] = jnp.zeros_like(acc_ref)

        acc_ref[...] += _dot(g_ref[...], w_ref[...], NT)

        @pl.when(r == nk - 1)
        def _():
            o_ref[...] = acc_ref[...]

    step = _nbytes((tm, tk), BF16) + _nbytes((tn, tk), BF16) + _nbytes((tm, tn), F32)
    return pl.pallas_call(
        body, name=name, grid=(T // tm, D // tn, nk),
        in_specs=[pl.BlockSpec((None, tm, tk), lambda i, j, r: (r // nkk, i, r % nkk)),
                  pl.BlockSpec((tn, tk), lambda i, j, r: (j, r))],
        out_specs=pl.BlockSpec((tm, tn), lambda i, j, r: (i, j)),
        out_shape=jax.ShapeDtypeStruct((T, D), F32),
        scratch_shapes=[pltpu.VMEM((tm, tn), F32)],
        compiler_params=_params(step, _nbytes((tm, tn), F32)),
    )(g3, w_std)


def _wgrad(name, grid, a, a_spec, gs, g_spec, out_shape, o_spec, acc_shape):
    n = len(gs)
    nt = grid[-1]

    def body(*refs):
        a_ref, g_refs, o_refs, acc_refs = refs[0], refs[1:1 + n], refs[1 + n:1 + 2 * n], refs[1 + 2 * n:]
        t = pl.program_id(len(grid) - 1)
        av = a_ref[...]
        for g_ref, o_ref, acc_ref in zip(g_refs, o_refs, acc_refs):
            @pl.when(t == 0)
            def _():
                acc_ref[...] = jnp.zeros_like(acc_ref)

            acc_ref[...] += _dot(av, g_ref[...], TN)

            @pl.when(t == nt - 1)
            def _():
                o_ref[...] = acc_ref[...]

    step = _nbytes(a_spec.block_shape, BF16) + n * (_nbytes(g_spec.block_shape, BF16) + _nbytes(acc_shape, F32))
    outs = pl.pallas_call(
        body, name=name, grid=grid,
        in_specs=[a_spec] + [g_spec] * n, out_specs=[o_spec] * n,
        out_shape=[jax.ShapeDtypeStruct(out_shape, F32)] * n,
        scratch_shapes=[pltpu.VMEM(acc_shape, F32)] * n,
        compiler_params=_params(step, n * _nbytes(acc_shape, F32)),
    )(a, *gs)
    return outs


def _wgrad_cols(name, a, gs):
    T, K = a.shape
    nb, _, ns = gs[0].shape
    tk, tt = _tile(K, 1024, LANES), _tile(T, WGRAD_TOKENS)
    return _wgrad(name, (nb, K // tk, T // tt), a,
                  pl.BlockSpec((tt, tk), lambda b, k, t: (t, k)), gs,
                  pl.BlockSpec((None, tt, ns), lambda b, k, t: (b, t, 0)),
                  (nb, K, ns), pl.BlockSpec((None, tk, ns), lambda b, k, t: (b, k, 0)), (tk, ns))


def _wgrad_rows(name, act, dx16):
    nb, T, fs = act.shape
    N = dx16.shape[1]
    tn, tt = _tile(N, 1024, LANES), _tile(T, WGRAD_TOKENS)
    return _wgrad(name, (nb, N // tn, T // tt), act,
                  pl.BlockSpec((None, tt, fs), lambda b, j, t: (b, t, 0)), [dx16],
                  pl.BlockSpec((tt, tn), lambda b, j, t: (t, j)),
                  (nb, fs, N), pl.BlockSpec((None, fs, tn), lambda b, j, t: (b, 0, j)), (fs, tn))[0]


def _wgrad_std(name, a, g):
    T, K = a.shape
    N = g.shape[1]
    tk, tn, tt = _tile(K, 1024, LANES), _tile(N, 1024, LANES), _tile(T, WGRAD_TOKENS)
    return _wgrad(name, (K // tk, N // tn, T // tt), a,
                  pl.BlockSpec((tt, tk), lambda k, j, t: (t, k)), [g],
                  pl.BlockSpec((tt, tn), lambda k, j, t: (t, j)),
                  (K, N), pl.BlockSpec((tk, tn), lambda k, j, t: (k, j)), (tk, tn))[0]


def _wgrad_sections(name, a, g3, ns):
    T, K = a.shape
    nsec, _, Ds = g3.shape
    cw = 256 if (ns % 256 == 0 and Ds % 256 == 0) else LANES
    per_sec, per_shard = Ds // cw, ns // cw
    nb = nsec * Ds // ns
    tk, tt = _tile(K, 1024, LANES), _tile(T, WGRAD_TOKENS_NARROW)
    return _wgrad(name, (K // tk, nsec * per_sec, T // tt), a,
                  pl.BlockSpec((tt, tk), lambda k, p, t: (t, k)), [g3],
                  pl.BlockSpec((None, tt, cw), lambda k, p, t: (p // per_sec, t, p % per_sec)),
                  (nb, K, ns), pl.BlockSpec((None, tk, cw), lambda k, p, t: (p // per_shard, k, p % per_shard)),
                  (tk, cw))[0]


def _tri2():
    r = lax.broadcasted_iota(jnp.int32, (LANES, 2 * LANES), 0)
    c = lax.broadcasted_iota(jnp.int32, (LANES, 2 * LANES), 1)
    return jnp.where((r >= c) | (c >= LANES), 1.0, 0.0).astype(BF16)


def _suffix_sums(v, tri):
    hi = v.astype(BF16)
    lo = (v - hi.astype(F32)).astype(BF16)
    both = _dot(hi, tri, NN) + _dot(lo, tri, NN)
    return both[:, :LANES], both[:, LANES:]


def _suffix_sums_wide(v, carry, tri):
    pieces = [None] * (v.shape[1] // LANES)
    for u in reversed(range(len(pieces))):
        incl, tot = _suffix_sums(v[:, u * LANES:(u + 1) * LANES], tri)
        pieces[u] = incl + carry
        carry = carry + tot
    return jnp.concatenate(pieces, axis=1), carry


def _sb_group(z, mask, carry, tri):
    lsn = -(jnp.maximum(z, 0.0) + jnp.log(1.0 + jnp.exp(-jnp.abs(z))))
    lk = lsn if mask is None else jnp.where(mask, lsn, 0.0)
    incl, carry = _suffix_sums_wide(lk, carry, tri)
    a = jnp.exp(z + lsn + (incl - lk))
    return (a if mask is None else jnp.where(mask, a, 0.0)), lsn, carry


def _head_norm_store(src_ref, g_ref, dst_ref, rows, chunk):
    def step(i, _):
        r0 = pl.multiple_of(i * chunk, chunk)
        v = src_ref[pl.ds(r0, chunk), :]
        r = lax.rsqrt(jnp.mean(v * v, axis=-1, keepdims=True) + EPS)
        dst_ref[pl.ds(r0, chunk), :] = (v * r * g_ref[...]).astype(dst_ref.dtype)
        return 0

    lax.fori_loop(0, rows // chunk, step, 0)


def _qkv_specs(T, ns, H):
    cps = ns // HEAD_DIM

    def spec(sec):
        return pl.BlockSpec((None, T, HEAD_DIM), lambda h: ((sec * H + h) // cps, 0, (sec * H + h) % cps))

    return [spec(0), spec(1), spec(2)]


def _attn_fwd(name, qkv, gq, gk, H):
    _, T, ns = qkv.shape
    D = H * HEAD_DIM
    BQ = _tile(T, ATTN_BLOCK, LANES)
    scale = HEAD_DIM ** -0.5

    def body(q_ref, k_ref, v_ref, gq_ref, gk_ref, o32_ref, o16_ref, qn, kn, vb):
        _head_norm_store(q_ref, gq_ref, qn, T, BQ)
        _head_norm_store(k_ref, gk_ref, kn, T, BQ)
        vb[...] = v_ref[...].astype(BF16)
        tri = _tri2()
        causal = lax.broadcasted_iota(jnp.int32, (BQ, BQ), 1) < lax.broadcasted_iota(jnp.int32, (BQ, BQ), 0)

        def qloop(qi, _):
            t0 = pl.multiple_of(qi * BQ, BQ)
            qb = qn[pl.ds(t0, BQ), :]

            def step(grp, carry, mask):
                o_acc, cr = carry
                s0 = pl.multiple_of(grp * BQ, BQ)
                z = _dot(qb, kn[pl.ds(s0, BQ), :], NT) * scale
                a, _, cr = _sb_group(z, mask, cr, tri)
                o_acc = o_acc + _dot(a.astype(BF16), vb[pl.ds(s0, BQ), :], NN)
                return o_acc, cr

            zero = jnp.zeros((BQ, LANES), F32)
            first = step(qi, (zero, zero), causal)
            o_acc, _ = lax.fori_loop(0, qi, lambda jj, c: step(qi - 1 - jj, c, None), first)
            o32_ref[pl.ds(t0, BQ), :] = o_acc
            o16_ref[pl.ds(t0, BQ), :] = o_acc.astype(BF16)
            return 0

        lax.fori_loop(0, T // BQ, qloop, 0)

    gspec = pl.BlockSpec((1, HEAD_DIM), lambda h: (0, 0))
    ospec = pl.BlockSpec((T, HEAD_DIM), lambda h: (0, h))
    step = 3 * _nbytes((T, HEAD_DIM), F32) + _nbytes((T, HEAD_DIM), F32) + _nbytes((T, HEAD_DIM), BF16)
    return pl.pallas_call(
        body, name=name, grid=(H,),
        in_specs=_qkv_specs(T, ns, H) + [gspec, gspec],
        out_specs=[ospec, ospec],
        out_shape=[jax.ShapeDtypeStruct((T, D), F32), jax.ShapeDtypeStruct((T, D), BF16)],
        scratch_shapes=[pltpu.VMEM((T, HEAD_DIM), BF16)] * 3,
        compiler_params=_params(step, 3 * _nbytes((T, HEAD_DIM), BF16) + 10 * _nbytes((BQ, BQ), F32)),
    )(qkv, qkv, qkv, gq, gk)


def _attn_bwd(name, qkv, gq, gk, do16, o32, H):
    _, T, ns = qkv.shape
    D = H * HEAD_DIM
    BQ = _tile(T, ATTN_BLOCK, LANES)
    scale = HEAD_DIM ** -0.5

    def body(q_ref, k_ref, v_ref, gq_ref, gk_ref, do_ref, o_ref, d3_ref, dgq_ref, dgk_ref,
             qn, kn, vb, dqn, dkn, dv):
        h = pl.program_id(0)
        _head_norm_store(q_ref, gq_ref, qn, T, BQ)
        _head_norm_store(k_ref, gk_ref, kn, T, BQ)
        vb[...] = v_ref[...].astype(BF16)
        dkn[...] = jnp.zeros_like(dkn)
        dv[...] = jnp.zeros_like(dv)
        tri = _tri2()
        causal = lax.broadcasted_iota(jnp.int32, (BQ, BQ), 1) < lax.broadcasted_iota(jnp.int32, (BQ, BQ), 0)

        def qloop(qi, _):
            t0 = pl.multiple_of(qi * BQ, BQ)
            qb = qn[pl.ds(t0, BQ), :]
            dob = do_ref[pl.ds(t0, BQ), :]
            total = jnp.sum(dob.astype(F32) * o_ref[pl.ds(t0, BQ), :], axis=-1, keepdims=True)

            def step(grp, carry, mask):
                dq_acc, cr, crd = carry
                s0 = pl.multiple_of(grp * BQ, BQ)
                kb = kn[pl.ds(s0, BQ), :]
                vj = vb[pl.ds(s0, BQ), :]
                z = _dot(qb, kb, NT) * scale
                a, lsn, cr = _sb_group(z, mask, cr, tri)
                ab = a.astype(BF16)
                dla = ab.astype(F32) * _dot(dob, vj, NT)
                later, crd = _suffix_sums_wide(dla, crd, tri)
                sig = jnp.exp(z + lsn)
                d_keep = (total - later) * sig
                dz = (dla * (1.0 - sig) - (d_keep if mask is None else jnp.where(mask, d_keep, 0.0))) * scale
                dzb = dz.astype(BF16)
                dq_acc = dq_acc + _dot(dzb, kb, NN)
                dkn[pl.ds(s0, BQ), :] += _dot(dzb, qb, TN)
                dv[pl.ds(s0, BQ), :] += _dot(ab, dob, TN)
                return dq_acc, cr, crd

            zero = jnp.zeros((BQ, LANES), F32)
            first = step(qi, (zero, zero, zero), causal)
            dq_acc, _, _ = lax.fori_loop(0, qi, lambda jj, c: step(qi - 1 - jj, c, None), first)
            dqn[pl.ds(t0, BQ), :] = dq_acc
            return 0

        lax.fori_loop(0, T // BQ, qloop, 0)

        @pl.when(h == 0)
        def _():
            dgq_ref[...] = jnp.zeros_like(dgq_ref)
            dgk_ref[...] = jnp.zeros_like(dgk_ref)

        def norm_bwd(src_ref, g_ref, dy_ref, sec, dg_ref):
            def step(i, _):
                r0 = pl.multiple_of(i * BQ, BQ)
                v = src_ref[pl.ds(r0, BQ), :]
                r = lax.rsqrt(jnp.mean(v * v, axis=-1, keepdims=True) + EPS)
                vh = v * r
                dyo = dy_ref[pl.ds(r0, BQ), :]
                dy = dyo * g_ref[...]
                m = jnp.mean(dy * vh, axis=-1, keepdims=True)
                d3_ref[sec, pl.ds(r0, BQ), :] = (r * (dy - vh * m)).astype(BF16)
                dg_ref[...] += jnp.sum(dyo * vh, axis=0, keepdims=True)
                return 0

            lax.fori_loop(0, T // BQ, step, 0)

        norm_bwd(q_ref, gq_ref, dqn, 0, dgq_ref)
        norm_bwd(k_ref, gk_ref, dkn, 1, dgk_ref)
        d3_ref[2] = dv[...].astype(BF16)

    gspec = pl.BlockSpec((1, HEAD_DIM), lambda h: (0, 0))
    hspec = pl.BlockSpec((T, HEAD_DIM), lambda h: (0, h))
    step = (3 * _nbytes((T, HEAD_DIM), F32) + _nbytes((T, HEAD_DIM), BF16) + _nbytes((T, HEAD_DIM), F32)
            + 3 * _nbytes((T, HEAD_DIM), BF16))
    scratch = 3 * _nbytes((T, HEAD_DIM), BF16) + 3 * _nbytes((T, HEAD_DIM), F32)
    return pl.pallas_call(
        body, name=name, grid=(H,),
        in_specs=_qkv_specs(T, ns, H) + [gspec, gspec, hspec, hspec],
        out_specs=[pl.BlockSpec((3, T, HEAD_DIM), lambda h: (0, 0, h)), gspec, gspec],
        out_shape=[jax.ShapeDtypeStruct((3, T, D), BF16), jax.ShapeDtypeStruct((1, HEAD_DIM), F32),
                   jax.ShapeDtypeStruct((1, HEAD_DIM), F32)],
        scratch_shapes=[pltpu.VMEM((T, HEAD_DIM), BF16)] * 3 + [pltpu.VMEM((T, HEAD_DIM), F32)] * 3,
        compiler_params=_params(step, scratch + 12 * _nbytes((BQ, BQ), F32)),
    )(qkv, qkv, qkv, gq, gk, do16, o32)


def _by_group(g, vals):
    out = vals[-1]
    for k in range(len(vals) - 2, -1, -1):
        out = jnp.where(g == k, vals[k], out)
    return out


def _pool_fwd(name, hf, x, wp, scale_row):
    T, D = x.shape
    G = len(POOL_WINDOWS)
    C = D // G
    tm = _tile(T, 512, POOL_HALO)
    hb = tm // POOL_HALO

    def body(h_ref, halo_ref, x_ref, w_ref, s_ref, xo_ref, p_ref):
        g, i = pl.program_id(0), pl.program_id(1)
        hv = h_ref[...]
        ext = jnp.concatenate([halo_ref[...] * (i > 0).astype(F32), hv], axis=0)
        sums, acc = [], ext
        for k in range(len(POOL_WINDOWS)):
            acc = acc + pltpu.roll(acc, 1 << k, 0)
            sums.append(acc)
        ws = _by_group(g, sums)[POOL_HALO:]
        t = i * tm + lax.broadcasted_iota(jnp.int32, (tm, 1), 0)
        cnt = jnp.minimum(t + 1, lax.shift_left(jnp.int32(2), g)).astype(F32)
        p = (ws / cnt - hv).astype(BF16)
        p_ref[...] = p
        xo_ref[...] = x_ref[...] + _dot(p, w_ref[...], NN) * s_ref[...]

    blk = pl.BlockSpec((tm, C), lambda g, i: (i, g))
    step = 3 * _nbytes((tm, C), F32) + _nbytes((tm, C), BF16) + _nbytes((C, C), BF16)
    return pl.pallas_call(
        body, name=name, grid=(G, T // tm),
        in_specs=[blk, pl.BlockSpec((POOL_HALO, C), lambda g, i: (jnp.maximum(i * hb - 1, 0), g)), blk,
                  pl.BlockSpec((None, C, C), lambda g, i: (g, 0, 0)), pl.BlockSpec((1, C), lambda g, i: (0, g))],
        out_specs=[blk, blk],
        out_shape=[jax.ShapeDtypeStruct((T, D), F32), jax.ShapeDtypeStruct((T, D), BF16)],
        compiler_params=_params(step + 6 * _nbytes((tm, C), F32)),
    )(hf, hf, x, wp, scale_row)


def _pool_bwd(name, dx, p, wp, scale_row):
    T, D = dx.shape
    G = len(POOL_WINDOWS)
    C = D // G
    tm = _tile(T, 512, POOL_HALO)
    hb = tm // POOL_HALO
    nt = T // tm
    n = tm + POOL_HALO

    def body(dx_ref, halo_ref, p_ref, w_ref, s_ref, dh_ref, dw_ref, ds_ref):
        g, i = pl.program_id(0), pl.program_id(1)
        dxv = dx_ref[...]
        dxe = jnp.concatenate([dxv, halo_ref[...] * (i < nt - 1).astype(F32)], axis=0)
        dyp = (dxe * s_ref[...]).astype(BF16)
        wv = w_ref[...]
        dp = _dot(dyp, wv, NT)
        t = i * tm + lax.broadcasted_iota(jnp.int32, (n, 1), 0)
        cnt = jnp.minimum(t + 1, lax.shift_left(jnp.int32(2), g)).astype(F32)
        sums, acc = [], dp / cnt
        for k in range(len(POOL_WINDOWS)):
            acc = acc + pltpu.roll(acc, n - (1 << k), 0)
            sums.append(acc)
        dh_ref[...] = _by_group(g, sums)[:tm] - dp[:tm]
        pv = p_ref[...]

        @pl.when(i == 0)
        def _():
            dw_ref[...] = jnp.zeros_like(dw_ref)
            ds_ref[...] = jnp.zeros_like(ds_ref)

        ds_ref[...] += jnp.sum(dxv * _dot(pv, wv, NN), axis=0, keepdims=True)
        dw_ref[...] += _dot(pv, dyp[:tm], TN)

    blk = pl.BlockSpec((tm, C), lambda g, i: (i, g))
    step = 2 * _nbytes((tm, C), F32) + _nbytes((tm, C), BF16) + _nbytes((C, C), BF16) + _nbytes((C, C), F32)
    return pl.pallas_call(
        body, name=name, grid=(G, nt),
        in_specs=[blk, pl.BlockSpec((POOL_HALO, C), lambda g, i: (jnp.minimum((i + 1) * hb, T // POOL_HALO - 1), g)),
                  blk, pl.BlockSpec((None, C, C), lambda g, i: (g, 0, 0)), pl.BlockSpec((1, C), lambda g, i: (0, g))],
        out_specs=[blk, pl.BlockSpec((None, C, C), lambda g, i: (g, 0, 0)), pl.BlockSpec((1, C), lambda g, i: (0, g))],
        out_shape=[jax.ShapeDtypeStruct((T, D), F32), jax.ShapeDtypeStruct((G, C, C), F32),
                   jax.ShapeDtypeStruct((1, D), F32)],
        compiler_params=_params(step + 8 * _nbytes((tm, C), F32)),
    )(dx, dx, p, wp, scale_row)


def _section_spec(rows, cw, ns, D, sec, row_map):
    per = ns // cw

    def imap(j, i):
        c = (sec * D) // cw + j
        return (c // per, row_map(i), c % per)

    return pl.BlockSpec((None, rows, cw), imap)


def _conv_fwd(name, bcx, cw_full):
    _, T, ns = bcx.shape
    D = cw_full.shape[1]
    cw = 256 if (ns % 256 == 0 and D % 256 == 0) else LANES
    tm = _tile(T, 512, CONV_HALO)
    hb = tm // CONV_HALO

    def body(b_ref, c_ref, u_ref, ch_ref, uh_ref, w_ref, o_ref):
        i = pl.program_id(1)
        gm = c_ref[...] * u_ref[...]
        ext = jnp.concatenate([ch_ref[...] * uh_ref[...] * (i > 0).astype(F32), gm], axis=0)
        w0, w1, w2 = w_ref[0:1, :], w_ref[1:2, :], w_ref[2:3, :]
        y = w2 * gm + w1 * pltpu.roll(ext, 1, 0)[CONV_HALO:] + w0 * pltpu.roll(ext, 2, 0)[CONV_HALO:]
        o_ref[...] = (b_ref[...] * y).astype(BF16)

    main = lambda i: i
    prev = lambda i: jnp.maximum(i * hb - 1, 0)
    return pl.pallas_call(
        body, name=name, grid=(D // cw, T // tm),
        in_specs=[_section_spec(tm, cw, ns, D, 0, main), _section_spec(tm, cw, ns, D, 1, main),
                  _section_spec(tm, cw, ns, D, 2, main), _section_spec(CONV_HALO, cw, ns, D, 1, prev),
                  _section_spec(CONV_HALO, cw, ns, D, 2, prev), pl.BlockSpec((3, cw), lambda j, i: (0, j))],
        out_specs=pl.BlockSpec((tm, cw), lambda j, i: (i, j)),
        out_shape=jax.ShapeDtypeStruct((T, D), BF16),
        compiler_params=_params(8 * _nbytes((tm, cw), F32)),
    )(bcx, bcx, bcx, bcx, bcx, cw_full)


def _conv_bwd(name, dby, bcx, cw_full):
    _, T, ns = bcx.shape
    D = cw_full.shape[1]
    cw = 256 if (ns % 256 == 0 and D % 256 == 0) else LANES
    tm = _tile(T, 512, CONV_HALO)
    hb = tm // CONV_HALO
    nt = T // tm
    n = tm + CONV_HALO

    def body(dby_ref, dbyh_ref, b_ref, bh_ref, c_ref, u_ref, ch_ref, uh_ref, w_ref, d3_ref, dw_ref):
        i = pl.program_id(1)
        w0, w1, w2 = w_ref[0:1, :], w_ref[1:2, :], w_ref[2:3, :]
        bv, cv, uv, dbyv = b_ref[...], c_ref[...], u_ref[...], dby_ref[...]
        gm = cv * uv
        ext_g = jnp.concatenate([ch_ref[...] * uh_ref[...] * (i > 0).astype(F32), gm], axis=0)
        g1 = pltpu.roll(ext_g, 1, 0)[CONV_HALO:]
        g2 = pltpu.roll(ext_g, 2, 0)[CONV_HALO:]
        y = w2 * gm + w1 * g1 + w0 * g2
        dy = dbyv * bv
        ext_dy = jnp.concatenate([dy, dbyh_ref[...] * bh_ref[...] * (i < nt - 1).astype(F32)], axis=0)
        dg = w2 * dy + w1 * pltpu.roll(ext_dy, n - 1, 0)[:tm] + w0 * pltpu.roll(ext_dy, n - 2, 0)[:tm]
        d3_ref[0] = (dbyv * y).astype(BF16)
        d3_ref[1] = (dg * uv).astype(BF16)
        d3_ref[2] = (dg * cv).astype(BF16)

        @pl.when(i == 0)
        def _():
            dw_ref[...] = jnp.zeros_like(dw_ref)

        rows = [jnp.sum(dy * v, axis=0, keepdims=True) for v in (g2, g1, gm)]
        dw_ref[...] += jnp.concatenate(rows + [jnp.zeros((8 - len(rows), cw), F32)], axis=0)

    main = lambda i: i
    prev = lambda i: jnp.maximum(i * hb - 1, 0)
    nxt = lambda i: jnp.minimum((i + 1) * hb, T // CONV_HALO - 1)
    return pl.pallas_call(
        body, name=name, grid=(D // cw, nt),
        in_specs=[pl.BlockSpec((tm, cw), lambda j, i: (i, j)), pl.BlockSpec((CONV_HALO, cw), lambda j, i: (nxt(i), j)),
                  _section_spec(tm, cw, ns, D, 0, main), _section_spec(CONV_HALO, cw, ns, D, 0, nxt),
                  _section_spec(tm, cw, ns, D, 1, main), _section_spec(tm, cw, ns, D, 2, main),
                  _section_spec(CONV_HALO, cw, ns, D, 1, prev), _section_spec(CONV_HALO, cw, ns, D, 2, prev),
                  pl.BlockSpec((3, cw), lambda j, i: (0, j))],
        out_specs=[pl.BlockSpec((3, tm, cw), lambda j, i: (0, i, j)), pl.BlockSpec((8, cw), lambda j, i: (0, j))],
        out_shape=[jax.ShapeDtypeStruct((3, T, D), BF16), jax.ShapeDtypeStruct((8, D), F32)],
        compiler_params=_params(14 * _nbytes((tm, cw), F32)),
    )(dby, dby, bcx, bcx, bcx, bcx, bcx, bcx, cw_full)


def _position():
    return lax.axis_index("x"), lax.axis_index("y"), lax.axis_index("c")


def _all_gather(name, tensors):
    n = len(tensors)
    shapes = [a.shape[1:] if idx is not None else a.shape for a, idx in tensors]

    def body(*refs):
        srcs, outs = refs[:n], refs[n:2 * n]
        send_sems, recv_sems, local_sems = refs[2 * n:]
        x, y, c = _position()
        me, sibling = (x, y, c), (x, y, 1 - c)
        chips = [(1 - x, y), (x, 1 - y), (1 - x, 1 - y)]

        def slot(out, p):
            return out.at[4 * p[0] + 2 * p[1] + p[2]]

        def copy(t, k, block, to, src=None):
            return pltpu.make_async_remote_copy(
                src_ref=slot(outs[t], block) if src is None else src, dst_ref=slot(outs[t], block),
                send_sem=send_sems.at[7 * t + k], recv_sem=recv_sems.at[7 * t + k],
                device_id=to, device_id_type=MESH)

        started, mine = [], []
        for t, (_, idx) in enumerate(tensors):
            src = srcs[t] if idx is None else srcs[t].at[idx]
            own = pltpu.make_async_copy(src, slot(outs[t], me), local_sems.at[t])
            own.start()
            mine.append(own)
            first = [copy(t, 0, me, sibling, src=src)]
            first += [copy(t, 1 + j, me, (*chip, c), src=src) for j, chip in enumerate(chips)]
            for cp in first:
                cp.start()
            started += first
        for t in range(n):
            for j, chip in enumerate(chips):
                copy(t, 1 + j, (*chip, c), me).wait_recv()
                passed = copy(t, 4 + j, (*chip, c), sibling)
                passed.start()
                started.append(passed)
        for t in range(n):
            copy(t, 0, sibling, me).wait_recv()
            for j, chip in enumerate(chips):
                copy(t, 4 + j, (*chip, 1 - c), me).wait_recv()
        for cp in started:
            cp.wait_send()
        for own in mine:
            own.wait()

    return pl.pallas_call(
        body, name=name,
        in_specs=[ANY] * n, out_specs=[ANY] * n,
        out_shape=[jax.ShapeDtypeStruct((N_DEV,) + tuple(s), a.dtype) for s, (a, _) in zip(shapes, tensors)],
        scratch_shapes=[pltpu.SemaphoreType.DMA((7 * n,)), pltpu.SemaphoreType.DMA((7 * n,)),
                        pltpu.SemaphoreType.DMA((n,))],
    )(*[a for a, _ in tensors])


def _core_copies(n):
    def copies(srcs, lands, send_sems, recv_sems):
        x, y, c = _position()
        out = []
        for t in range(n):
            for k in range(N_XY):
                out.append(pltpu.make_async_remote_copy(
                    src_ref=srcs[t].at[2 * k + 1 - c], dst_ref=lands[t].at[k],
                    send_sem=send_sems.at[N_XY * t + k], recv_sem=recv_sems.at[N_XY * t + k],
                    device_id=(x, y, 1 - c), device_id_type=MESH))
        return out

    return copies, N_XY * n


def _chip_partial(name, g, got, core):
    _, R, C = g.shape
    tr = _row_tile(R, C, 4, 2 << 20)

    def body(c_ref, a_ref, b_ref, o_ref):
        o_ref[...] = (a_ref[...] + b_ref[...]).astype(BF16)

    blk = pl.BlockSpec((None, tr, C), lambda k, i, c: (k, i, 0))
    return pl.pallas_call(
        body, name=name,
        grid_spec=pltpu.PrefetchScalarGridSpec(
            num_scalar_prefetch=1, grid=(N_XY, R // tr),
            in_specs=[pl.BlockSpec((None, tr, C), lambda k, i, c: (2 * k + c[0], i, 0)), blk], out_specs=blk),
        out_shape=jax.ShapeDtypeStruct(got.shape, BF16),
        compiler_params=_params(3 * _nbytes((tr, C), F32)),
    )(core, g, got)


def _chip_copies(n):
    def copies(srcs, lands, send_sems, recv_sems):
        x, y, c = _position()
        mine = 2 * x + y
        out = []
        for t in range(n):
            for d in range(1, N_XY):
                px, py = x ^ (d >> 1), y ^ (d & 1)
                out.append(pltpu.make_async_remote_copy(
                    src_ref=srcs[t].at[2 * px + py], dst_ref=lands[t].at[mine],
                    send_sem=send_sems.at[3 * t + d - 1], recv_sem=recv_sems.at[3 * t + d - 1],
                    device_id=(px, py, c), device_id_type=MESH))
        return out

    return copies, 3 * n


def _gather_copies(tensors):
    def copies(srcs, lands, send_sems, recv_sems):
        x, y, c = _position()
        me = 4 * x + 2 * y + c
        out = []
        for t, (_, idx) in enumerate(tensors):
            for d in range(1, N_DEV):
                out.append(pltpu.make_async_remote_copy(
                    src_ref=srcs[t].at[idx], dst_ref=lands[t].at[me],
                    send_sem=send_sems.at[7 * t + d - 1], recv_sem=recv_sems.at[7 * t + d - 1],
                    device_id=(x ^ (d >> 2), y ^ ((d >> 1) & 1), c ^ (d & 1)), device_id_type=MESH))
        return out

    return copies, 7 * len(tensors)


HBM = pl.BlockSpec(memory_space=pltpu.HBM)
SEM = pl.BlockSpec(memory_space=pltpu.SEMAPHORE)
EFFECT = pltpu.SideEffectType.DATAFLOW_SIDE_EFFECTING


def _exchange_start(name, exchange, srcs, lands, after):
    copies, n_sem = exchange
    n = len(srcs)

    def body(*refs):
        for cp in copies(refs[:n], refs[n:2 * n], refs[2 * n + 1], refs[2 * n + 2]):
            cp.start()
        refs[-1][...] = jnp.zeros_like(refs[-1])

    operands = [pltpu.with_memory_space_constraint(a, pltpu.HBM) for a in list(srcs) + list(lands)]
    outs = pl.pallas_call(
        body, name=name, in_specs=[HBM] * (2 * n) + [ANY],
        out_specs=[SEM, SEM] + [HBM] * n + [pl.BlockSpec(memory_space=pltpu.VMEM)],
        out_shape=[pltpu.SemaphoreType.DMA((n_sem,)), pltpu.SemaphoreType.DMA((n_sem,))]
        + [pltpu.HBM(a.shape, a.dtype) for a in lands] + [jax.ShapeDtypeStruct((8, LANES), F32)],
        input_output_aliases={n + k: 2 + k for k in range(n)},
        compiler_params=pltpu.CompilerParams(has_side_effects=EFFECT),
    )(*operands, after)
    return outs[0], outs[1], outs[2:2 + n], outs[-1]


def _exchange_wait(name, exchange, send_sems, recv_sems, srcs, lands, after):
    copies, _ = exchange
    n = len(srcs)

    def body(*refs):
        for cp in copies(refs[:n], refs[n:2 * n], refs[2 * n], refs[2 * n + 1]):
            cp.wait_send()
            cp.wait_recv()

    return pl.pallas_call(
        body, name=name, in_specs=[HBM] * (2 * n) + [SEM, SEM, ANY], out_specs=[HBM] * n,
        out_shape=[pltpu.HBM(a.shape, a.dtype) for a in lands],
        input_output_aliases={n + k: k for k in range(n)},
        compiler_params=pltpu.CompilerParams(has_side_effects=EFFECT),
    )(*srcs, *lands, send_sems, recv_sems, after)


def _adamw(w, g, m, v):
    m = ADAM_B1 * m + (1.0 - ADAM_B1) * g
    v = ADAM_B2 * v + (1.0 - ADAM_B2) * (g * g)
    m_hat = m / (1.0 - ADAM_B1 ** ADAM_STEP)
    v_hat = v / (1.0 - ADAM_B2 ** ADAM_STEP)
    return -ADAM_LR * (m_hat / (jnp.sqrt(v_hat) + ADAM_EPS) + ADAM_WD * w), m, v


def _reduce_update(name, own, landed, chip, w, m, v, layer, prev):
    _, R, C = landed.shape
    L = w.shape[0]
    tr = _row_tile(R, C, 4, 1 << 20, 16)

    def body(chip_ref, own_ref, p_ref, w_ref, m_ref, v_ref, *rest):
        g_ref, d_ref, mo_ref, vo_ref = rest[-4:]
        mine = chip_ref[0]
        g = None
        for k in range(N_XY):
            term = jnp.where(mine == k, own_ref[...], p_ref[k]).astype(F32)
            g = term if g is None else g + term
        g_ref[...] = g
        d_ref[...], mo_ref[...], vo_ref[...] = _adamw(w_ref[...], g, m_ref[...], v_ref[...])

    lay = pl.BlockSpec((None, tr, C), lambda i, c: (layer, i, 0))
    osh = jax.ShapeDtypeStruct((L, R, C), F32)
    extra = [] if prev is None else list(prev)
    return pl.pallas_call(
        body, name=name,
        grid_spec=pltpu.PrefetchScalarGridSpec(
            num_scalar_prefetch=1, grid=(R // tr,),
            in_specs=[pl.BlockSpec((None, tr, C), lambda i, c: (c[0], i, 0)),
                      pl.BlockSpec((N_XY, tr, C), lambda i, c: (0, i, 0)), lay, lay, lay] + [ANY] * len(extra),
            out_specs=[lay] * 4),
        out_shape=[osh] * 4,
        input_output_aliases={6 + k: k for k in range(len(extra))},
        compiler_params=_params(9 * _nbytes((tr, C), F32)),
    )(chip, own, landed, w, m, v, *extra)


def _all_reduce_small(name, vec):
    R = vec.shape[0]

    def body(v_ref, o_ref, tot_ref, buf, send_sems, recv_sems):
        x, y, c = _position()
        me = 4 * x + 2 * y + c
        buf[me] = v_ref[...]
        sends = []
        for d in range(1, N_DEV):
            cp = pltpu.make_async_remote_copy(
                src_ref=v_ref, dst_ref=buf.at[me], send_sem=send_sems.at[d - 1], recv_sem=recv_sems.at[d - 1],
                device_id=(x ^ (d >> 2), y ^ ((d >> 1) & 1), c ^ (d & 1)), device_id_type=MESH)
            cp.start()
            sends.append(cp)
        for cp in sends:
            cp.wait()
        s = buf[0]
        for k in range(1, N_DEV):
            s = s + buf[k]
        o_ref[...] = s
        tot_ref[...] = jnp.sum(jnp.sum(s[0:8], axis=0, keepdims=True), axis=1, keepdims=True)

    vm = pl.BlockSpec(memory_space=pltpu.VMEM)
    return pl.pallas_call(
        body, name=name, in_specs=[vm], out_specs=[vm, vm],
        out_shape=[jax.ShapeDtypeStruct(vec.shape, F32), jax.ShapeDtypeStruct((1, 1), F32)],
        scratch_shapes=[pltpu.VMEM((N_DEV, R, LANES), F32), pltpu.SemaphoreType.DMA((N_DEV - 1,)),
                        pltpu.SemaphoreType.DMA((N_DEV - 1,))],
    )(vec)


def _adamw_small(name, w, g, m, v):
    def body(w_ref, g_ref, m_ref, v_ref, d_ref, mo_ref, vo_ref):
        d_ref[...], mo_ref[...], vo_ref[...] = _adamw(w_ref[...], g_ref[...], m_ref[...], v_ref[...])

    vm = pl.BlockSpec(memory_space=pltpu.VMEM)
    return pl.pallas_call(
        body, name=name, in_specs=[vm] * 4, out_specs=[vm] * 3,
        out_shape=[jax.ShapeDtypeStruct(w.shape, F32)] * 3,
    )(w, g, m, v)


def _pack(parts, rows):
    flat = jnp.concatenate([p.reshape(-1) for p in parts])
    return jnp.pad(flat, (0, rows * LANES - flat.shape[0])).reshape(rows, LANES)


def _unpack(packed, shapes, skip=0):
    flat, out, off = packed.reshape(-1), [], skip
    for s in shapes:
        n = 1
        for d in s:
            n *= d
        out.append(flat[off:off + n].reshape(s))
        off += n
    return out


def kernel(x, norm_mix_g, norm_ffn_g, sb_w_qkv, sb_g_q, sb_g_k, sb_w_o, pool_w, pool_scale, conv_w_in, conv_w, conv_w_out, ffn_w_gate, ffn_w_up, ffn_w_down, loss_target, m_norm_mix_g, m_norm_ffn_g, m_sb_w_qkv, m_sb_g_q, m_sb_g_k, m_sb_w_o, m_pool_w, m_pool_scale, m_conv_w_in, m_conv_w, m_conv_w_out, m_ffn_w_gate, m_ffn_w_up, m_ffn_w_down, v_norm_mix_g, v_norm_ffn_g, v_sb_w_qkv, v_sb_g_q, v_sb_g_k, v_sb_w_o, v_pool_w, v_pool_scale, v_conv_w_in, v_conv_w, v_conv_w_out, v_ffn_w_gate, v_ffn_w_up, v_ffn_w_down):
    _, T, D = x.shape
    depth = norm_mix_g.shape[0]
    H = D // HEAD_DIM
    G = len(POOL_WINDOWS)
    pool_rows = pool_w.shape[2]
    xs = x[0]

    big = {
        "sb_w_qkv": (sb_w_qkv, m_sb_w_qkv, v_sb_w_qkv), "sb_w_o": (sb_w_o, m_sb_w_o, v_sb_w_o),
        "pool_w": tuple(a.reshape(a.shape[0], G * pool_rows, a.shape[3]) for a in (pool_w, m_pool_w, v_pool_w)),
        "conv_w_in": (conv_w_in, m_conv_w_in, v_conv_w_in), "conv_w_out": (conv_w_out, m_conv_w_out, v_conv_w_out),
        "ffn_w_gate": (ffn_w_gate, m_ffn_w_gate, v_ffn_w_gate), "ffn_w_up": (ffn_w_up, m_ffn_w_up, v_ffn_w_up),
        "ffn_w_down": (ffn_w_down, m_ffn_w_down, v_ffn_w_down),
    }
    w16 = {k: t[0].astype(BF16) for k, t in big.items()}

    me = 4 * lax.axis_index("x") + 2 * lax.axis_index("y") + lax.axis_index("c")

    def layer_tensors(i):
        kind, j = i % 3, i // 3
        names = list([("sb_w_qkv", "sb_w_o"), ("pool_w",), ("conv_w_in", "conv_w_out")][kind])
        tensors = [(w16[nm], j) for nm in names] + [(w16[nm], i) for nm in ("ffn_w_gate", "ffn_w_up", "ffn_w_down")]
        names += ["ffn_w_gate", "ffn_w_up", "ffn_w_down"]
        if kind == 2:
            tensors.append((conv_w, j))
            names.append("conv_w")
        return names, tensors

    def gather_start(i, part, after):
        names, tensors = layer_tensors(i)
        keep = [k for k, nm in enumerate(names) if nm.startswith("ffn_") == (part == "ffn")]
        names, tensors = [names[k] for k in keep], [tensors[k] for k in keep]
        lands = [lax.dynamic_update_slice(lax.empty((N_DEV,) + a.shape[1:], a.dtype), a[idx][None], (me, 0, 0))
                 for a, idx in tensors]
        exchange, srcs = _gather_copies(tensors), [a for a, _ in tensors]
        return (names, exchange, srcs) + _exchange_start(f"gather_start_{part}_l{i}", exchange, srcs, lands, after)

    def gather_wait(i, part, pending, after):
        names, exchange, srcs, send_sems, recv_sems, lands, _ = pending
        return dict(zip(names, _exchange_wait(f"gather_wait_{part}_l{i}", exchange, send_sems, recv_sems, srcs, lands, after)))

    names0, tensors0 = layer_tensors(0)
    gathered = [dict(zip(names0, _all_gather("gather_l0", tensors0)))] + [None] * (depth - 1)

    def std_cols(wb):
        return jnp.transpose(wb, (1, 0, 2)).reshape(wb.shape[1], -1)

    saved = []
    xc = xs
    in_flight = {}
    for i in range(depth):
        kind, j = i % 3, i // 3
        if (i, "mix") in in_flight:
            gathered[i] = gather_wait(i, "mix", in_flight.pop((i, "mix")), xc)
        gw = gathered[i]
        if i == 0:
            last = gw["ffn_w_down"]
            for nxt in range(1, depth):
                for part in ("mix", "ffn"):
                    in_flight[(nxt, part)] = gather_start(nxt, part, last)
                    last = in_flight[(nxt, part)][-1]
        follows = tuple(p[-1] for p in in_flight.values())
        s = {"x_in": xc}
        if kind == 0:
            h = _rms_fwd(f"norm_mix_l{i}", xc, norm_mix_g[i:i + 1], BF16, follows)
            qkv = _mm_cols(f"qkv_l{i}", h, gw["sb_w_qkv"], F32)
            o32, o16 = _attn_fwd(f"attn_fwd_l{i}", qkv, sb_g_q[j:j + 1], sb_g_k[j:j + 1], H)
            xc = _mm_res(f"attn_out_l{i}", o16, gw["sb_w_o"].reshape(D, D), xc)
            s.update(h=h, qkv=qkv, o32=o32, o16=o16)
        elif kind == 1:
            hf = _rms_fwd(f"norm_mix_l{i}", xc, norm_mix_g[i:i + 1], F32, follows)
            wp = jnp.transpose(gw["pool_w"].reshape(N_DEV, G, pool_rows, D // G), (1, 0, 2, 3)).reshape(G, D // G, D // G)
            xc, p = _pool_fwd(f"pool_fwd_l{i}", hf, xc, wp, pool_scale[j:j + 1])
            s.update(p=p, wp=wp)
        else:
            h = _rms_fwd(f"norm_mix_l{i}", xc, norm_mix_g[i:i + 1], BF16, follows)
            bcx =_mm_cols(f"conv_in_l{i}", h, gw["conv_w_in"], F32)
            cw_full = jnp.transpose(gw["conv_w"], (1, 0, 2)).reshape(3, D)
            by = _conv_fwd(f"conv_fwd_l{i}", bcx, cw_full)
            xc = _mm_res(f"conv_out_l{i}", by, gw["conv_w_out"].reshape(D, D), xc)
            s.update(h=h, bcx=bcx, by=by, cw_full=cw_full)
        s["x_mid"] = xc
        if (i, "ffn") in in_flight:
            gw.update(gather_wait(i, "ffn", in_flight.pop((i, "ffn")), xc))
        h2 = _rms_fwd(f"norm_ffn_l{i}", xc, norm_ffn_g[i:i + 1], BF16)
        gate, up, act = _ffn_up(f"ffn_up_l{i}", h2, gw["ffn_w_gate"], gw["ffn_w_up"])
        xc = _ffn_down(f"ffn_down_l{i}", act, gw["ffn_w_down"], xc)
        s.update(h2=h2, gate=gate, up=up, act=act)
        saved.append(s)

    dx, dx16, loss_part = _loss_head("loss_head", xc, loss_target[0])

    outs = {k: None for k in big}
    core = lax.axis_index("c").astype(jnp.int32).reshape(1)
    chip = (2 * lax.axis_index("x") + lax.axis_index("y")).astype(jnp.int32).reshape(1)

    def core_start(tag, grads):
        names = list(grads)
        arrays = [grads[nm][0] for nm in names]
        exchange = _core_copies(len(names))
        lands = [lax.empty((N_XY,) + g.shape[1:], g.dtype) for g in arrays]
        started = _exchange_start(f"swap_core_start_{tag}", exchange, arrays, lands, arrays[0])
        return (tag, [(nm, grads[nm][1]) for nm in names], exchange, arrays) + started

    def chip_start(pending, after):
        tag, layers, exchange, arrays, send_sems, recv_sems, lands, _ = pending
        got = _exchange_wait(f"swap_core_wait_{tag}", exchange, send_sems, recv_sems, arrays, lands, after)
        partial = [_chip_partial(f"chip_sum_{nm}_{tag}", g, b, core) for (nm, _), g, b in zip(layers, arrays, got)]
        exchange = _chip_copies(len(layers))
        lands = [lax.empty(p.shape, p.dtype) for p in partial]
        started = _exchange_start(f"swap_chip_start_{tag}", exchange, partial, lands, got[0])
        return (tag, layers, exchange, partial) + started

    def reduce_finish(pending, after):
        tag, layers, exchange, srcs, send_sems, recv_sems, lands, _ = pending
        lands = _exchange_wait(f"swap_chip_wait_{tag}", exchange, send_sems, recv_sems, srcs, lands, after)
        for (nm, layer), own, landed in zip(layers, srcs, lands):
            w, m, v = big[nm]
            outs[nm] = _reduce_update(f"update_{nm}_{tag}", own, landed, chip, w, m, v, layer, outs[nm])

    def behind(small_operand, token):
        return small_operand + token[0, 0]

    d_mix, d_ffn = [None] * depth, [None] * depth
    small = {}
    mix_core = None
    for i in reversed(range(depth)):
        kind, j = i % 3, i // 3
        gw, s = gathered[i], saved[i]
        grads = {}
        dgate, dup = _ffn_dact(f"ffn_dact_l{i}", dx16, gw["ffn_w_down"], s["gate"], s["up"],
                               () if mix_core is None else (mix_core[-1],))
        dwd = _wgrad_rows(f"ffn_dwd_l{i}", s["act"], dx16)
        mix_chip = None if mix_core is None else chip_start(mix_core, dwd)
        dh2 = _ffn_dh(f"ffn_dh_l{i}", dgate, dup, gw["ffn_w_gate"], gw["ffn_w_up"],
                      () if mix_chip is None else (mix_chip[-1],))
        dwg, dwu = _wgrad_cols(f"ffn_dwgu_l{i}", s["h2"], [dgate, dup])
        if mix_chip is not None:
            reduce_finish(mix_chip, dwu)
        ffn_core = core_start(f"ffn_l{i}", {"ffn_w_down": (dwd, i), "ffn_w_gate": (dwg, i), "ffn_w_up": (dwu, i)})
        dx, dx16, d_ffn[i] = _rms_bwd(f"norm_ffn_bwd_l{i}", dh2, s["x_mid"], norm_ffn_g[i:i + 1], dx,
                                      (ffn_core[-1],))
        g_mix_row = norm_mix_g[i:i + 1]
        if kind == 0:
            wo = gw["sb_w_o"]
            do16 = _mm_nt_rows(f"attn_do_l{i}", dx16, wo, BF16)
            dwo = _wgrad_std(f"attn_dwo_l{i}", s["o16"], dx16)
            grads["sb_w_o"] = (dwo.reshape(wo.shape), j)
            ffn_chip = chip_start(ffn_core, dwo)
            d3, dgq, dgk = _attn_bwd(f"attn_bwd_l{i}", s["qkv"], behind(sb_g_q[j:j + 1], ffn_chip[-1]),
                                     sb_g_k[j:j + 1], do16, s["o32"], H)
            small[("sb_g_q", j)], small[("sb_g_k", j)] = dgq, dgk
            dh = _mm_nt_sections(f"qkv_dh_l{i}", d3, std_cols(gw["sb_w_qkv"]))
            grads["sb_w_qkv"] = (_wgrad_sections(f"qkv_dw_l{i}", s["h"], d3, gw["sb_w_qkv"].shape[2]), j)
        elif kind == 1:
            dh, dwp, dps = _pool_bwd(f"pool_bwd_l{i}", dx, s["p"], s["wp"], pool_scale[j:j + 1])
            small[("pool_scale", j)] = dps
            dwp = jnp.transpose(dwp.reshape(G, N_DEV, pool_rows, D // G), (1, 0, 2, 3)).reshape(N_DEV, G * pool_rows, D // G)
            grads["pool_w"] = (dwp, j)
            ffn_chip = chip_start(ffn_core, dps)
            g_mix_row = behind(g_mix_row, ffn_chip[-1])
        else:
            wout = gw["conv_w_out"]
            dby = _mm_nt_rows(f"conv_dby_l{i}", dx16, wout, F32)
            dwout = _wgrad_std(f"conv_dwout_l{i}", s["by"], dx16)
            grads["conv_w_out"] = (dwout.reshape(wout.shape), j)
            ffn_chip = chip_start(ffn_core, dwout)
            d3, dcw = _conv_bwd(f"conv_bwd_l{i}", dby, s["bcx"], behind(s["cw_full"], ffn_chip[-1]))
            small[("conv_w", j)] = dcw[0:3]
            dh = _mm_nt_sections(f"conv_dh_l{i}", d3, std_cols(gw["conv_w_in"]))
            grads["conv_w_in"] = (_wgrad_sections(f"conv_dwin_l{i}", s["h"], d3, gw["conv_w_in"].shape[2]), j)
        dx, dx16, d_mix[i] = _rms_bwd(f"norm_mix_bwd_l{i}", dh, s["x_in"], g_mix_row, dx)
        reduce_finish(ffn_chip, d_mix[i])
        mix_core = core_start(f"mix_l{i}", grads)
    mix_chip = chip_start(mix_core, mix_core[-1])

    n_sb, n_pool, n_conv = sb_g_q.shape[0], pool_scale.shape[0], conv_w.shape[0]
    pieces = [loss_part, jnp.concatenate(d_mix), jnp.concatenate(d_ffn),
              jnp.concatenate([small[("sb_g_q", j)] for j in range(n_sb)]),
              jnp.concatenate([small[("sb_g_k", j)] for j in range(n_sb)]),
              jnp.concatenate([small[("pool_scale", j)] for j in range(n_pool)]),
              jnp.stack([small[("conv_w", j)] for j in range(n_conv)])]
    n_small = sum(p.size for p in pieces)
    rows = -(-n_small // (8 * LANES)) * 8
    summed, loss = _all_reduce_small("reduce_small", _pack(pieces, rows))
    shapes = [norm_mix_g.shape, norm_ffn_g.shape, sb_g_q.shape, sb_g_k.shape, pool_scale.shape, (n_conv, 3, D)]
    g_mix, g_ffn, g_q, g_k, g_ps, g_cw_full = _unpack(summed, shapes, skip=8 * LANES)
    cshard = conv_w.shape[2]
    me = 4 * lax.axis_index("x") + 2 * lax.axis_index("y") + lax.axis_index("c")
    g_cw = lax.dynamic_slice_in_dim(g_cw_full, me * cshard, cshard, axis=2)
    small_names = ["norm_mix_g", "norm_ffn_g", "sb_g_q", "sb_g_k", "pool_scale", "conv_w"]
    small_w = [norm_mix_g, norm_ffn_g, sb_g_q, sb_g_k, pool_scale, conv_w]
    small_m = [m_norm_mix_g, m_norm_ffn_g, m_sb_g_q, m_sb_g_k, m_pool_scale, m_conv_w]
    small_v = [v_norm_mix_g, v_norm_ffn_g, v_sb_g_q, v_sb_g_k, v_pool_scale, v_conv_w]
    small_g = [g_mix, g_ffn, g_q, g_k, g_ps, g_cw]
    n_upd = sum(a.size for a in small_w)
    urows = -(-n_upd // (8 * LANES)) * 8
    sd, sm, sv = _adamw_small("update_small", _pack(small_w, urows), _pack(small_g, urows),
                              _pack(small_m, urows), _pack(small_v, urows))
    reduce_finish(mix_chip, sd)
    sshapes = [a.shape for a in small_w]
    res = {nm: (g, d, m_, v_) for nm, g, d, m_, v_ in
           zip(small_names, small_g, _unpack(sd, sshapes), _unpack(sm, sshapes), _unpack(sv, sshapes))}
    for nm, stacks in outs.items():
        shape = pool_w.shape if nm == "pool_w" else big[nm][0].shape
        res[nm] = tuple(a.reshape(shape) for a in stacks)

    order = ["norm_mix_g", "norm_ffn_g", "sb_w_qkv", "sb_g_q", "sb_g_k", "sb_w_o", "pool_w", "pool_scale",
             "conv_w_in", "conv_w", "conv_w_out", "ffn_w_gate", "ffn_w_up", "ffn_w_down"]
    return (loss.reshape(()), dx[None], *[res[nm][0] for nm in order], *[res[nm][1] for nm in order],
            *[res[nm][2] for nm in order], *[res[nm][3] for nm in order])
```
